```python
import math
import jax, jax.numpy as jnp
from jax import lax
import numpy as np

D_MODEL = 1024
BATCH = 8
SEQ = 8192
DEPTH = 1

EPS = 1e-6
GLA_HEADS = 4
GLA_DK = D_MODEL // 16
GLA_DV = D_MODEL // 8
GLA_GATE_RANK = 16
GLA_GATE_NORM = 16.0
GLA_CHUNK = 64
DIFF_HEADS = 4
DIFF_DH = D_MODEL // 16
DIFF_DV = 2 * DIFF_DH
Q_BLOCK = 128
REL_BUCKETS = 32
REL_MAX_DIST = 128
N_EXPERTS = 32
TOP_K = 4
D_FF = D_MODEL
SWIGLU_LIMIT = 7.0
SWIGLU_ALPHA = 1.702
MOE_BLOCK = 512
GLA_QK_W = GLA_HEADS * GLA_DK
GLA_V_W = GLA_HEADS * GLA_DV
DIFF_QK_W = DIFF_HEADS * 2 * DIFF_DH
DIFF_V_W = DIFF_HEADS * DIFF_DV
D_MIX = GLA_V_W + DIFF_V_W
IN_SIZES = (GLA_QK_W, GLA_QK_W, GLA_V_W, GLA_GATE_RANK, GLA_V_W, DIFF_QK_W, DIFF_QK_W, DIFF_V_W)
IN_WIDTH = sum(IN_SIZES)

kernel_name = 'hymba_gla_diffattn_moe_block'


def _split_points():
    pts, acc = [], 0
    for s in IN_SIZES[:-1]:
        acc += s
        pts.append(acc)
    return pts


def _rmsnorm(x, gain):
    xf = x.astype(jnp.float32)
    y = xf * lax.rsqrt(jnp.mean(xf * xf, axis=-1, keepdims=True) + EPS)
    return (y * gain.astype(jnp.float32)).astype(x.dtype)


def _t5_causal_bucket(rel):
    n = jnp.maximum(-rel, 0)
    max_exact = REL_BUCKETS // 2
    nf = jnp.maximum(n, 1).astype(jnp.float32)
    large = max_exact + (jnp.log(nf / max_exact) / math.log(REL_MAX_DIST / max_exact)
                         * (REL_BUCKETS - max_exact)).astype(jnp.int32)
    large = jnp.minimum(large, REL_BUCKETS - 1)
    return jnp.where(n < max_exact, n, large)


def _gla_group(q, k, v, g_low, r, w_gate2, b_gate2, out_gain):
    B, S, _ = q.shape
    C = GLA_CHUNK
    N = S // C
    f32 = jnp.float32
    log_a = jax.nn.log_sigmoid((g_low @ w_gate2 + b_gate2).astype(f32)) / GLA_GATE_NORM

    def heads(t, d):
        return t.reshape(B, N, C, GLA_HEADS, d).transpose(0, 3, 1, 2, 4).astype(f32)

    qh = heads(q, GLA_DK) * (GLA_DK ** -0.5)
    kh = heads(k, GLA_DK)
    vh = heads(v, GLA_DV)
    b = jnp.cumsum(heads(log_a, GLA_DK), axis=3)
    b_last = b[:, :, :, -1:, :]
    q_dec = qh * jnp.exp(b)
    att = jnp.einsum('bhntd,bhnsd->bhnts', q_dec, kh * jnp.exp(-b))
    causal = jnp.tril(jnp.ones((C, C), dtype=bool))
    o_intra = jnp.einsum('bhnts,bhnse->bhnte', jnp.where(causal, att, 0.0), vh)
    kv = jnp.einsum('bhnsd,bhnse->bhnde', kh * jnp.exp(b_last - b), vh)
    decay = jnp.exp(b_last[:, :, :, 0, :])

    def step(state, inp):
        kv_n, dec_n = inp
        return state * dec_n[..., None] + kv_n, state

    init = jnp.zeros((B, GLA_HEADS, GLA_DK, GLA_DV), f32)
    _, states = lax.scan(step, init, (jnp.moveaxis(kv, 2, 0), jnp.moveaxis(decay, 2, 0)))
    states = jnp.moveaxis(states, 0, 2)
    o = o_intra + jnp.einsum('bhntd,bhnde->bhnte', q_dec, states)
    o = o.transpose(0, 2, 3, 1, 4).reshape(B, S, GLA_HEADS, GLA_DV)
    o = _rmsnorm(o, out_gain).reshape(B, S, GLA_V_W)
    return o.astype(r.dtype) * jax.nn.silu(r)


def _diff_group(q, k, v, q_gain, k_gain, lam_q1, lam_k1, lam_q2, lam_k2, sub_gain, rel_bias, lam_init):
    B, S, _ = q.shape
    H, d = DIFF_HEADS, DIFF_DH
    f32 = jnp.float32
    q = _rmsnorm(q.reshape(B, S, H, 2, d), q_gain)
    k = _rmsnorm(k.reshape(B, S, H, 2, d), k_gain)
    v = v.reshape(B, S, H, DIFF_DV)
    lam = (jnp.exp(jnp.sum(lam_q1.astype(f32) * lam_k1.astype(f32)))
           - jnp.exp(jnp.sum(lam_q2.astype(f32) * lam_k2.astype(f32))) + lam_init)
    scale = d ** -0.5
    nq = S // Q_BLOCK
    q_blocks = q.reshape(B, nq, Q_BLOCK, H, 2, d).transpose(1, 0, 2, 3, 4, 5)
    k_pos = jnp.arange(S)

    def block(args):
        q_blk, i = args
        q_pos = i * Q_BLOCK + jnp.arange(Q_BLOCK)
        rel = k_pos[None, :] - q_pos[:, None]
        bias = rel_bias[_t5_causal_bucket(rel)].transpose(2, 0, 1).astype(f32)
        s = jnp.einsum('bqhcd,bkhcd->bhcqk', q_blk, k, preferred_element_type=f32) * scale
        s = s + bias[None, :, None]
        s = jnp.where(rel <= 0, s, -1e30)
        p = jax.nn.softmax(s, axis=-1)
        a = p[:, :, 0] - lam * p[:, :, 1]
        return jnp.einsum('bhqk,bkhe->bqhe', a.astype(v.dtype), v)

    o = lax.map(block, (q_blocks, jnp.arange(nq)))
    o = o.transpose(1, 0, 2, 3, 4).reshape(B, S, H, DIFF_DV)
    o = _rmsnorm(o, sub_gain) * (1.0 - lam_init)
    return o.reshape(B, S, DIFF_V_W).astype(v.dtype)


def _moe(h, w_router, b_router, w_gate_up, b_gate_up, w_down, b_down):
    B, S, D = h.shape
    T = B * S
    A = T * TOP_K
    f32 = jnp.float32
    xt = h.reshape(T, D)
    logits = (xt @ w_router + b_router).astype(f32)
    top_val, top_idx = lax.top_k(logits, TOP_K)
    gates = jax.nn.softmax(top_val, axis=-1)
    e_flat = top_idx.reshape(-1)
    tok_flat = jnp.repeat(jnp.arange(T, dtype=jnp.int32), TOP_K)
    g_flat = gates.reshape(-1)
    order = jnp.argsort(e_flat)
    e_sorted = e_flat[order]
    counts = jnp.bincount(e_flat, length=N_EXPERTS)
    starts = jnp.cumsum(counts) - counts
    padded = (counts + MOE_BLOCK - 1) // MOE_BLOCK * MOE_BLOCK
    pad_end = jnp.cumsum(padded)
    pad_start = pad_end - padded
    dest = pad_start[e_sorted] + (jnp.arange(A) - starts[e_sorted])
    n_blocks = -(-A // MOE_BLOCK) + N_EXPERTS
    P = n_blocks * MOE_BLOCK
    slot_tok = jnp.full((P,), T, jnp.int32).at[dest].set(tok_flat[order])
    slot_gate = jnp.zeros((P,), f32).at[dest].set(g_flat[order])
    block_exp = jnp.minimum(jnp.searchsorted(pad_end, jnp.arange(n_blocks) * MOE_BLOCK, side='right'),
                            N_EXPERTS - 1)
    x_pad = jnp.concatenate([xt, jnp.zeros((1, D), xt.dtype)], axis=0)

    def expert_block(args):
        e, toks, g = args
        gu = x_pad[toks] @ w_gate_up[e] + b_gate_up[e]
        x_glu, x_lin = gu[:, :D_FF], gu[:, D_FF:]
        x_glu = jnp.minimum(x_glu, SWIGLU_LIMIT)
        x_lin = jnp.clip(x_lin, -SWIGLU_LIMIT, SWIGLU_LIMIT)
        hid = x_glu * jax.nn.sigmoid(SWIGLU_ALPHA * x_glu) * (x_lin + 1.0)
        y = hid @ w_down[e] + b_down[e]
        return y * g[:, None].astype(y.dtype)

    ys = lax.map(expert_block, (block_exp, slot_tok.reshape(n_blocks, MOE_BLOCK),
                                slot_gate.reshape(n_blocks, MOE_BLOCK)))
    out = jax.ops.segment_sum(ys.reshape(P, D), slot_tok, num_segments=T + 1)[:T]
    return out.reshape(B, S, D).astype(h.dtype)


def setup_inputs(seed: int = 0) -> dict:
    key = jax.random.key(seed)
    ks = jax.random.split(key, 24)
    L = DEPTH
    f32 = jnp.float32

    def w(k, shape, fan_in):
        return jax.random.normal(k, shape, f32) * (fan_in ** -0.5)

    def gain(k, shape):
        return 1.0 + 0.02 * jax.random.normal(k, shape, f32)

    def small(k, shape, s):
        return s * jax.random.normal(k, shape, f32)

    return {
        'x': jax.random.normal(ks[0], (BATCH, SEQ, D_MODEL), f32),
        'norm1_gain': gain(ks[1], (L, D_MODEL)),
        'w_in': w(ks[2], (L, D_MODEL, IN_WIDTH), D_MODEL),
        'w_gla_gate2': w(ks[3], (L, GLA_GATE_RANK, GLA_QK_W), GLA_GATE_RANK),
        'b_gla_gate2': small(ks[4], (L, GLA_QK_W), 0.1),
        'gla_out_gain': gain(ks[5], (L, GLA_DV)),
        'diff_q_gain': gain(ks[6], (L, DIFF_DH)),
        'diff_k_gain': gain(ks[7], (L, DIFF_DH)),
        'lam_q1': small(ks[8], (L, DIFF_DH), 0.1),
        'lam_k1': small(ks[9], (L, DIFF_DH), 0.1),
        'lam_q2': small(ks[10], (L, DIFF_DH), 0.1),
        'lam_k2': small(ks[11], (L, DIFF_DH), 0.1),
        'diff_sub_gain': gain(ks[12], (L, DIFF_DV)),
        'rel_bias': small(ks[13], (REL_BUCKETS, DIFF_HEADS), 0.5),
        'w_out': w(ks[14], (L, D_MIX, D_MODEL), D_MIX),
        'norm2_gain': gain(ks[15], (L, D_MODEL)),
        'w_router': w(ks[16], (L, D_MODEL, N_EXPERTS), D_MODEL),
        'b_router': small(ks[17], (L, N_EXPERTS), 0.01),
        'w_gate_up': w(ks[18], (L, N_EXPERTS, D_MODEL, 2 * D_FF), D_MODEL),
        'b_gate_up': small(ks[19], (L, N_EXPERTS, 2 * D_FF), 0.01),
        'w_down': w(ks[20], (L, N_EXPERTS, D_FF, D_MODEL), D_FF),
        'b_down': small(ks[21], (L, N_EXPERTS, D_MODEL), 0.01),
    }


def reference(x, norm1_gain, w_in, w_gla_gate2, b_gla_gate2, gla_out_gain, diff_q_gain, diff_k_gain,
              lam_q1, lam_k1, lam_q2, lam_k2, diff_sub_gain, rel_bias, w_out, norm2_gain,
              w_router, b_router, w_gate_up, b_gate_up, w_down, b_down):
    h = x
    pts = _split_points()
    for l in range(DEPTH):
        lam_init = 0.8 - 0.6 * math.exp(-0.3 * l)
        n = _rmsnorm(h, norm1_gain[l])
        proj = n @ w_in[l]
        gq, gk, gv, glow, gr, dq, dk, dv = jnp.split(proj, pts, axis=-1)
        y_gla = _gla_group(gq, gk, gv, glow, gr, w_gla_gate2[l], b_gla_gate2[l], gla_out_gain[l])
        y_diff = _diff_group(dq, dk, dv, diff_q_gain[l], diff_k_gain[l], lam_q1[l], lam_k1[l],
                             lam_q2[l], lam_k2[l], diff_sub_gain[l], rel_bias, lam_init)
        h = h + jnp.concatenate([y_gla, y_diff], axis=-1) @ w_out[l]
        m = _rmsnorm(h, norm2_gain[l])
        h = h + _moe(m, w_router[l], b_router[l], w_gate_up[l], b_gate_up[l], w_down[l], b_down[l])
    return h
```

```python
import functools
import math

import jax
import jax.numpy as jnp
from jax import lax
from jax.experimental import pallas as pl
from jax.experimental.pallas import tpu as pltpu

F32 = jnp.float32
BF16 = jnp.bfloat16
I32 = jnp.int32

EPS = 1e-6
GLA_HEADS = 4
GLA_DK = 64
GLA_DV = 128
GLA_RANK = 16
GLA_GATE_NORM = 16.0
GLA_CHUNK = 64
DIFF_HEADS = 4
DIFF_DH = 64
DIFF_DV = 128
REL_BUCKETS = 32
REL_MAX_DIST = 128
N_EXPERTS = 32
TOP_K = 4
SWIGLU_LIMIT = 7.0
SWIGLU_ALPHA = 1.702
MOE_BLOCK = 512
NEG_BIG = -1e30

VMEM_LIMIT = 56 * 1024 * 1024


def _cparams(sem):
    return pltpu.CompilerParams(dimension_semantics=sem, vmem_limit_bytes=VMEM_LIMIT)


def _dot(a, b):
    return jnp.dot(a, b, preferred_element_type=F32)


def _dot_nt(a, b):
    return lax.dot_general(a, b, (((1,), (1,)), ((), ())), preferred_element_type=F32)


def _dot_tn(a, b):
    return lax.dot_general(a, b, (((0,), (0,)), ((), ())), preferred_element_type=F32)


def _split_bf16(x):
    hi = x.astype(BF16)
    lo = (x - hi.astype(F32)).astype(BF16)
    return hi, lo


def _proj_kernel(x_ref, g1_ref, wm_ref, wlow_ref, wg2_ref, bg2_ref, bd_ref, qg_ref, kg_ref,
                 gq_ref, gk_ref, gv_ref, gr_ref, la_ref, dq_ref, dk_ref, dv_ref, wf_ref):
    @pl.when(pl.program_id(0) == 0)
    def _():
        wf_ref[...] = _dot(wlow_ref[...].astype(BF16), wg2_ref[...].astype(BF16)).astype(BF16)

    x = x_ref[...]
    ms = jnp.mean(x * x, axis=-1, keepdims=True)
    n = (x * lax.rsqrt(ms + EPS) * g1_ref[...]).astype(BF16)

    def mm(a, b):
        return _dot(n, wm_ref[:, a:b])

    gq_ref[...] = (mm(0, 256) * (GLA_DK ** -0.5)).astype(BF16)
    gk_ref[...] = mm(256, 512).astype(BF16)
    gv_ref[...] = mm(512, 1024).astype(BF16)
    gr_ref[...] = mm(1024, 1536).astype(BF16)

    z = _dot(n, wf_ref[...]) + bg2_ref[...]
    la_ref[...] = (jnp.minimum(z, 0.0) - jnp.log(1.0 + jnp.exp(-jnp.abs(z)))) * (1.0 / GLA_GATE_NORM)

    def qknorm(y, gain):
        ss = _dot((y * y).astype(BF16), bd_ref[...])
        return y * lax.rsqrt(ss * (1.0 / DIFF_DH) + EPS) * gain

    dq_ref[...] = (qknorm(mm(1536, 2048), qg_ref[...]) * (DIFF_DH ** -0.5)).astype(BF16)
    dk_ref[...] = qknorm(mm(2048, 2560), kg_ref[...]).astype(BF16)
    dv_ref[...] = mm(2560, 3072).astype(BF16)


def _proj(x2, g1, wm, wlow, wg2, bg2, bd, qg, kg, tm):
    T, D = x2.shape
    const = lambda i: (0, 0)
    row = lambda i: (i, 0)
    outs = [(256, BF16), (256, BF16), (512, BF16), (512, BF16), (256, F32),
            (512, BF16), (512, BF16), (512, BF16)]
    return pl.pallas_call(
        _proj_kernel,
        grid=(T // tm,),
        in_specs=[
            pl.BlockSpec((tm, D), row),
            pl.BlockSpec((1, D), const),
            pl.BlockSpec(wm.shape, const),
            pl.BlockSpec(wlow.shape, const),
            pl.BlockSpec(wg2.shape, const),
            pl.BlockSpec(bg2.shape, const),
            pl.BlockSpec(bd.shape, const),
            pl.BlockSpec(qg.shape, const),
            pl.BlockSpec(kg.shape, const),
        ],
        out_specs=[pl.BlockSpec((tm, w), row) for w, _ in outs],
        out_shape=[jax.ShapeDtypeStruct((T, w), dt) for w, dt in outs],
        scratch_shapes=[pltpu.VMEM((D, 256), BF16)],
        compiler_params=_cparams(("arbitrary",)),
        name="proj",
    )(x2, g1, wm, wlow, wg2, bg2, bd, qg, kg)


def _gla_kernel(q_ref, k_ref, v_ref, r_ref, la_ref, gain_ref, o_ref, state_ref, *, tg):
    C = GLA_CHUNK

    @pl.when(pl.program_id(1) == 0)
    def _():
        state_ref[...] = jnp.zeros(state_ref.shape, F32)

    ri = lax.broadcasted_iota(I32, (C, C), 0)
    ci = lax.broadcasted_iota(I32, (C, C), 1)
    causal = ri >= ci
    tril = jnp.where(causal, 1.0, 0.0).astype(BF16)
    ones_cv = jnp.ones((C, GLA_DV), BF16)
    gain = gain_ref[...]

    for c in range(tg // C):
        sl = slice(c * C, (c + 1) * C)
        la_hi, la_lo = _split_bf16(la_ref[sl, :])
        b = _dot(tril, la_hi) + _dot(tril, la_lo)
        b_last = b[C - 1:C, :]
        q = q_ref[sl, :].astype(F32)
        k = k_ref[sl, :].astype(F32)
        q_dec = (q * jnp.exp(b)).astype(BF16)
        k_dec = (k * jnp.exp(-b)).astype(BF16)
        k_last = (k * jnp.exp(b_last - b)).astype(BF16)
        for h in range(GLA_HEADS):
            hs = slice(h * GLA_DK, (h + 1) * GLA_DK)
            vs = slice(h * GLA_DV, (h + 1) * GLA_DV)
            v = v_ref[sl, vs]
            att = jnp.where(causal, _dot_nt(q_dec[:, hs], k_dec[:, hs]), 0.0)
            state = state_ref[h]
            o = _dot(att.astype(BF16), v) + _dot(q_dec[:, hs], state.astype(BF16))
            dcol = _dot_tn(la_hi[:, hs], ones_cv) + _dot_tn(la_lo[:, hs], ones_cv)
            state_ref[h] = state * jnp.exp(dcol) + _dot_tn(k_last[:, hs], v)
            on = o * lax.rsqrt(jnp.mean(o * o, axis=-1, keepdims=True) + EPS) * gain
            r = r_ref[sl, vs].astype(F32)
            o_ref[sl, vs] = (on * (r * jax.nn.sigmoid(r))).astype(o_ref.dtype)


def _gla(gq, gk, gv, gr, la, gain, B, S, tg):
    nt = S // tg
    row = lambda b, t: (b * nt + t, 0)
    const = lambda b, t: (0, 0)
    return pl.pallas_call(
        functools.partial(_gla_kernel, tg=tg),
        grid=(B, nt),
        in_specs=[
            pl.BlockSpec((tg, 256), row),
            pl.BlockSpec((tg, 256), row),
            pl.BlockSpec((tg, 512), row),
            pl.BlockSpec((tg, 512), row),
            pl.BlockSpec((tg, 256), row),
            pl.BlockSpec((1, GLA_DV), const),
        ],
        out_specs=pl.BlockSpec((tg, 512), row),
        out_shape=jax.ShapeDtypeStruct((B * S, 512), BF16),
        scratch_shapes=[pltpu.VMEM((GLA_HEADS, GLA_DK, GLA_DV), F32)],
        compiler_params=_cparams(("arbitrary", "arbitrary")),
        name="gla",
    )(gq, gk, gv, gr, la, gain)


def _diff_kernel(far_ref, q_ref, k_ref, v_ref, bias_ref, lq1_ref, lk1_ref, lq2_ref, lk2_ref,
                 sg_ref, o_ref, *, t, lam_init):
    h = pl.program_id(1)
    i = pl.program_id(2)
    q = q_ref[...]
    lane = lax.broadcasted_iota(I32, q.shape, 1)
    zero = jnp.zeros_like(q)
    qs = jnp.concatenate([jnp.where(lane < DIFF_DH, q, zero),
                          jnp.where(lane >= DIFF_DH, q, zero)], axis=0)

    def scores(j):
        kj = k_ref[pl.ds(pl.multiple_of(j * t, t), t), :]
        return _dot_nt(qs, kj)

    def values(j):
        return v_ref[pl.ds(pl.multiple_of(j * t, t), t), :]

    s = scores(i) + jnp.concatenate([bias_ref[0], bias_ref[0]], axis=0)
    m = jnp.max(s, axis=-1, keepdims=True)
    p = jnp.exp(s - m)
    l = jnp.sum(p, axis=-1, keepdims=True)
    acc = _dot(p.astype(BF16), values(i))

    def update(carry, s, vj):
        m, l, acc = carry
        m_new = jnp.maximum(m, jnp.max(s, axis=-1, keepdims=True))
        alpha = jnp.exp(m - m_new)
        p = jnp.exp(s - m_new)
        l = alpha * l + jnp.sum(p, axis=-1, keepdims=True)
        acc = alpha * acc + _dot(p.astype(BF16), vj)
        return m_new, l, acc

    def sub_diag(carry):
        j = i - 1
        s = scores(j) + jnp.concatenate([bias_ref[1], bias_ref[1]], axis=0)
        return update(carry, s, values(j))

    m, l, acc = lax.cond(i > 0, sub_diag, lambda c: c, (m, l, acc))

    far_bias = far_ref[h]

    def far(j, carry):
        return update(carry, scores(j) + far_bias, values(j))

    m, l, acc = lax.fori_loop(0, jnp.maximum(i - 1, 0), far, (m, l, acc))

    lam = (jnp.exp(jnp.sum(lq1_ref[...] * lk1_ref[...], axis=-1, keepdims=True))
           - jnp.exp(jnp.sum(lq2_ref[...] * lk2_ref[...], axis=-1, keepdims=True)) + lam_init)
    out = acc / l
    o = out[:t] - lam * out[t:]
    on = o * lax.rsqrt(jnp.mean(o * o, axis=-1, keepdims=True) + EPS) * sg_ref[...]
    o_ref[...] = (on * (1.0 - lam_init)).astype(o_ref.dtype)


def _diff(far_bias, dq, dk, dv, bias_tiles, lq1, lk1, lq2, lk2, sg, B, S, t, lam_init):
    nq = S // t
    vec = lambda b, h, i: (0, 0)
    return pl.pallas_call(
        functools.partial(_diff_kernel, t=t, lam_init=lam_init),
        grid=(B, DIFF_HEADS, nq),
        in_specs=[
            pl.BlockSpec(memory_space=pltpu.SMEM),
            pl.BlockSpec((None, t, 128), lambda b, h, i: (b, i, h)),
            pl.BlockSpec((None, S, 128), lambda b, h, i: (b, 0, h)),
            pl.BlockSpec((None, S, 128), lambda b, h, i: (b, 0, h)),
            pl.BlockSpec((None, 2, t, t), lambda b, h, i: (h, 0, 0, 0)),
            pl.BlockSpec((1, DIFF_DH), vec),
            pl.BlockSpec((1, DIFF_DH), vec),
            pl.BlockSpec((1, DIFF_DH), vec),
            pl.BlockSpec((1, DIFF_DH), vec),
            pl.BlockSpec((1, DIFF_DV), vec),
        ],
        out_specs=pl.BlockSpec((None, t, 128), lambda b, h, i: (b, i, h)),
        out_shape=jax.ShapeDtypeStruct((B, S, 512), BF16),
        compiler_params=_cparams(("arbitrary", "arbitrary", "arbitrary")),
        name="diff_attn",
    )(far_bias, dq, dk, dv, bias_tiles, lq1, lk1, lq2, lk2, sg)


def _t5_causal_bucket(n):
    max_exact = REL_BUCKETS // 2
    nf = jnp.maximum(n, 1).astype(F32)
    large = max_exact + (jnp.log(nf / max_exact) / math.log(REL_MAX_DIST / max_exact)
                         * (REL_BUCKETS - max_exact)).astype(I32)
    large = jnp.minimum(large, REL_BUCKETS - 1)
    return jnp.where(n < max_exact, n, large)


def _bias_tiles(rel_bias, t):
    qi = jnp.arange(t)[:, None]
    kj = jnp.arange(t)[None, :]
    d0 = qi - kj
    diag = jnp.where((d0 >= 0)[..., None], rel_bias[_t5_causal_bucket(jnp.maximum(d0, 0))], NEG_BIG)
    sub = rel_bias[_t5_causal_bucket(d0 + t)]
    return jnp.stack([diag, sub], axis=0).transpose(3, 0, 1, 2).astype(F32)


def _mix_kernel(x_ref, yg_ref, yd_ref, wo_ref, g2_ref, wr_ref, br_ref,
                h_ref, m_ref, e_ref, gate_ref, rank_ref, cnt_ref, base_ref, *, tm):
    @pl.when(pl.program_id(0) == 0)
    def _():
        base_ref[...] = jnp.zeros(base_ref.shape, F32)

    h1 = x_ref[...] + _dot(yg_ref[...], wo_ref[0:512, :]) + _dot(yd_ref[...], wo_ref[512:1024, :])
    h_ref[...] = h1
    ms = jnp.mean(h1 * h1, axis=-1, keepdims=True)
    m = h1 * lax.rsqrt(ms + EPS) * g2_ref[...]
    m_ref[...] = m

    m_hi, m_lo = _split_bf16(m)
    w_hi, w_lo = _split_bf16(wr_ref[...])
    logits = _dot_nt(w_hi, m_hi) + _dot_nt(w_hi, m_lo) + _dot_nt(w_lo, m_hi) + br_ref[...]

    eidx = lax.broadcasted_iota(I32, logits.shape, 0)
    work = logits
    vals, idxs = [], []
    for _ in range(TOP_K):
        mx = jnp.max(work, axis=0, keepdims=True)
        ix = jnp.min(jnp.where(work == mx, eidx, N_EXPERTS), axis=0, keepdims=True)
        vals.append(mx)
        idxs.append(ix)
        work = jnp.where(eidx == ix, -jnp.inf, work)
    ex = [jnp.exp(v - vals[0]) for v in vals]
    den = ex[0] + ex[1] + ex[2] + ex[3]
    gate_ref[...] = jnp.concatenate([e / den for e in ex], axis=0)
    e_ref[...] = jnp.concatenate(idxs, axis=0)

    onehots = [jnp.where(eidx == ix, 1.0, 0.0) for ix in idxs]
    cnt = onehots[0] + onehots[1] + onehots[2] + onehots[3]
    ti = lax.broadcasted_iota(I32, (tm, tm), 0)
    tj = lax.broadcasted_iota(I32, (tm, tm), 1)
    upper = jnp.where(ti < tj, 1.0, 0.0).astype(BF16)
    prefix = _dot(cnt.astype(BF16), upper) + base_ref[...]
    rank_ref[...] = jnp.concatenate(
        [jnp.sum(oh * prefix, axis=0, keepdims=True) for oh in onehots], axis=0).astype(I32)
    base_ref[...] = base_ref[...] + jnp.sum(cnt, axis=1, keepdims=True)
    cnt_ref[...] = base_ref[...]


def _mix(x2, yg, yd, wo, g2, wr_t, br, tm):
    T, D = x2.shape
    row = lambda i: (i, 0)
    col = lambda i: (0, i)
    const = lambda i: (0, 0)
    return pl.pallas_call(
        functools.partial(_mix_kernel, tm=tm),
        grid=(T // tm,),
        in_specs=[
            pl.BlockSpec((tm, D), row),
            pl.BlockSpec((tm, 512), row),
            pl.BlockSpec((tm, 512), row),
            pl.BlockSpec(wo.shape, const),
            pl.BlockSpec((1, D), const),
            pl.BlockSpec(wr_t.shape, const),
            pl.BlockSpec(br.shape, const),
        ],
        out_specs=[
            pl.BlockSpec((tm, D), row),
            pl.BlockSpec((tm, D), row),
            pl.BlockSpec((TOP_K, tm), col),
            pl.BlockSpec((TOP_K, tm), col),
            pl.BlockSpec((TOP_K, tm), col),
            pl.BlockSpec((N_EXPERTS, 1), const),
        ],
        out_shape=[
            jax.ShapeDtypeStruct((T, D), F32),
            jax.ShapeDtypeStruct((T, D), F32),
            jax.ShapeDtypeStruct((TOP_K, T), I32),
            jax.ShapeDtypeStruct((TOP_K, T), F32),
            jax.ShapeDtypeStruct((TOP_K, T), I32),
            jax.ShapeDtypeStruct((N_EXPERTS, 1), F32),
        ],
        scratch_shapes=[pltpu.VMEM((N_EXPERTS, 1), F32)],
        compiler_params=_cparams(("arbitrary",)),
        name="mix_router",
    )(x2, yg, yd, wo, g2, wr_t, br)


def _row_copy_wait_all(src_ref, dst_ref, sem, n):
    def body(r, c):
        pltpu.make_async_copy(src_ref.at[pl.ds(0, 1)], dst_ref.at[pl.ds(0, 1)], sem).wait()
        return c
    lax.fori_loop(0, n, body, 0)


def _dispatch_kernel(dest_hbm, m_ref, xs_in, xs_hbm, dest_smem, sem_idx, sem, *, tm):
    del xs_in
    i = pl.program_id(0)
    cp = pltpu.make_async_copy(dest_hbm.at[i], dest_smem, sem_idx)
    cp.start()
    cp.wait()

    def body(r, c):
        for k in range(TOP_K):
            d = dest_smem[k * tm + r]
            pltpu.make_async_copy(m_ref.at[pl.ds(r, 1)], xs_hbm.at[pl.ds(d, 1)], sem).start()
        return c
    lax.fori_loop(0, tm, body, 0)
    _row_copy_wait_all(m_ref, xs_hbm, sem, TOP_K * tm)


def _dispatch(dest_tiles, m, xs_zero, tm):
    T, D = m.shape
    return pl.pallas_call(
        functools.partial(_dispatch_kernel, tm=tm),
        grid=(T // tm,),
        in_specs=[
            pl.BlockSpec(memory_space=pl.ANY),
            pl.BlockSpec((tm, D), lambda i: (i, 0)),
            pl.BlockSpec(memory_space=pl.ANY),
        ],
        out_specs=pl.BlockSpec(memory_space=pl.ANY),
        out_shape=jax.ShapeDtypeStruct(xs_zero.shape, xs_zero.dtype),
        scratch_shapes=[pltpu.SMEM((TOP_K * tm,), I32), pltpu.SemaphoreType.DMA,
                        pltpu.SemaphoreType.DMA],
        input_output_aliases={2: 0},
        compiler_params=_cparams(("arbitrary",)),
        name="moe_dispatch",
    )(dest_tiles, m, xs_zero)


def _experts_kernel(bexp_ref, nused_ref, xs_ref, wgu_ref, bgu_ref, wd_ref, bd_ref, ys_ref, *, dff):
    del bexp_ref

    @pl.when(pl.program_id(0) < nused_ref[0])
    def _():
        x = xs_ref[...].astype(BF16)
        gu = _dot(x, wgu_ref[...]) + bgu_ref[...]
        glu = jnp.minimum(gu[:, :dff], SWIGLU_LIMIT)
        lin = jnp.clip(gu[:, dff:], -SWIGLU_LIMIT, SWIGLU_LIMIT)
        hid = glu * jax.nn.sigmoid(SWIGLU_ALPHA * glu) * (lin + 1.0)
        ys_ref[...] = _dot(hid.astype(BF16), wd_ref[...]) + bd_ref[...]


def _experts(block_exp, n_used, xs, wgu, bgu, wd, bd):
    P, D = xs.shape
    dff = wd.shape[1]
    nb = P // MOE_BLOCK
    blk = lambda i, be, nu: (jnp.minimum(i, nu[0] - 1), 0)
    exp3 = lambda i, be, nu: (be[i], 0, 0)
    return pl.pallas_call(
        functools.partial(_experts_kernel, dff=dff),
        grid_spec=pltpu.PrefetchScalarGridSpec(
            num_scalar_prefetch=2,
            grid=(nb,),
            in_specs=[
                pl.BlockSpec((MOE_BLOCK, D), blk),
                pl.BlockSpec((None, D, 2 * dff), exp3),
                pl.BlockSpec((None, 1, 2 * dff), exp3),
                pl.BlockSpec((None, dff, D), exp3),
                pl.BlockSpec((None, 1, D), exp3),
            ],
            out_specs=pl.BlockSpec((MOE_BLOCK, D), blk),
        ),
        out_shape=jax.ShapeDtypeStruct((P, D), F32),
        compiler_params=_cparams(("arbitrary",)),
        name="moe_experts",
    )(block_exp, n_used, xs, wgu, bgu, wd, bd)


def _combine_kernel(dest_hbm, h_ref, gate_ref, ys_hbm, o_ref, buf, dest_smem, sem_idx, sem, *, tm):
    i = pl.program_id(0)
    cp = pltpu.make_async_copy(dest_hbm.at[i], dest_smem, sem_idx)
    cp.start()
    cp.wait()

    def body(r, c):
        for k in range(TOP_K):
            d = dest_smem[k * tm + r]
            pltpu.make_async_copy(ys_hbm.at[pl.ds(d, 1)], buf.at[k, pl.ds(r, 1)], sem).start()
        return c
    lax.fori_loop(0, tm, body, 0)
    _row_copy_wait_all(ys_hbm, buf.at[0], sem, TOP_K * tm)

    g = gate_ref[...]
    acc = h_ref[...]
    for k in range(TOP_K):
        acc = acc + g[:, k:k + 1] * buf[k]
    o_ref[...] = acc


def _combine(dest_tiles, h1, gates_t, ys, tm):
    T, D = h1.shape
    return pl.pallas_call(
        functools.partial(_combine_kernel, tm=tm),
        grid=(T // tm,),
        in_specs=[
            pl.BlockSpec(memory_space=pl.ANY),
            pl.BlockSpec((tm, D), lambda i: (i, 0)),
            pl.BlockSpec((tm, TOP_K), lambda i: (i, 0)),
            pl.BlockSpec(memory_space=pl.ANY),
        ],
        out_specs=pl.BlockSpec((tm, D), lambda i: (i, 0)),
        out_shape=jax.ShapeDtypeStruct((T, D), F32),
        scratch_shapes=[pltpu.VMEM((TOP_K, tm, D), F32), pltpu.SMEM((TOP_K * tm,), I32),
                        pltpu.SemaphoreType.DMA, pltpu.SemaphoreType.DMA],
        compiler_params=_cparams(("arbitrary",)),
        name="moe_combine",
    )(dest_tiles, h1, gates_t, ys)


def _tile(n, pref):
    t = min(n, pref)
    assert n % t == 0, (n, pref)
    return t


def _layer(h, l, p, rel_bias):
    B, S, D = h.shape
    T = B * S
    x2 = h.reshape(T, D)
    lam_init = 0.8 - 0.6 * math.exp(-0.3 * l)

    w_in = p['w_in']
    c_low = 1024
    wm = jnp.concatenate([w_in[:, :c_low], w_in[:, c_low + GLA_RANK:]], axis=1).astype(BF16)
    wlow = jnp.pad(w_in[:, c_low:c_low + GLA_RANK], ((0, 0), (0, 128 - GLA_RANK)))
    wg2 = jnp.pad(p['w_gla_gate2'], ((0, 128 - GLA_RANK), (0, 0)))
    bg2 = p['b_gla_gate2'].reshape(1, -1)
    gidx = jnp.arange(512) // DIFF_DH
    bd = (gidx[:, None] == gidx[None, :]).astype(BF16)
    qg = jnp.tile(p['diff_q_gain'], 512 // DIFF_DH).reshape(1, 512)
    kg = jnp.tile(p['diff_k_gain'], 512 // DIFF_DH).reshape(1, 512)

    tm = _tile(T, 512)
    gq, gk, gv, gr, la, dq, dk, dv = _proj(
        x2, p['norm1_gain'].reshape(1, D), wm, wlow, wg2, bg2, bd, qg, kg, tm)

    y_gla = _gla(gq, gk, gv, gr, la, p['gla_out_gain'].reshape(1, GLA_DV), B, S, _tile(S, 256))

    t = _tile(S, 256)
    assert t >= 128
    y_diff = _diff(rel_bias[REL_BUCKETS - 1].astype(F32),
                   dq.reshape(B, S, 512), dk.reshape(B, S, 512), dv.reshape(B, S, 512),
                   _bias_tiles(rel_bias, t),
                   p['lam_q1'].reshape(1, -1), p['lam_k1'].reshape(1, -1),
                   p['lam_q2'].reshape(1, -1), p['lam_k2'].reshape(1, -1),
                   p['diff_sub_gain'].reshape(1, -1), B, S, t, lam_init).reshape(T, 512)

    h1, m, e_t, gate_t, rank_t, counts = _mix(
        x2, y_gla, y_diff, p['w_out'].astype(BF16), p['norm2_gain'].reshape(1, D),
        p['w_router'].T, p['b_router'].reshape(-1, 1), tm)

    counts = counts.reshape(-1).astype(I32)
    padded = (counts + MOE_BLOCK - 1) // MOE_BLOCK * MOE_BLOCK
    pad_end = jnp.cumsum(padded)
    pad_start = pad_end - padded
    n_blocks = -(-(T * TOP_K) // MOE_BLOCK) + N_EXPERTS
    P = n_blocks * MOE_BLOCK
    block_exp = jnp.minimum(
        jnp.searchsorted(pad_end, jnp.arange(n_blocks, dtype=I32) * MOE_BLOCK, side='right'),
        N_EXPERTS - 1).astype(I32)
    n_used = (pad_end[-1:] // MOE_BLOCK).astype(I32)
    dest = pad_start[e_t] + rank_t

    tmd = _tile(T, 256)
    dest_tiles = dest.reshape(TOP_K, T // tmd, tmd).transpose(1, 0, 2).reshape(T // tmd, TOP_K * tmd)

    xs = _dispatch(dest_tiles, m, jnp.zeros((P, D), F32), tmd)
    ys = _experts(block_exp, n_used, xs, p['w_gate_up'].astype(BF16),
                  p['b_gate_up'].reshape(N_EXPERTS, 1, -1), p['w_down'].astype(BF16),
                  p['b_down'].reshape(N_EXPERTS, 1, -1))
    out = _combine(dest_tiles, h1, gate_t.T, ys, tmd)
    return out.reshape(B, S, D)


def kernel(x, norm1_gain, w_in, w_gla_gate2, b_gla_gate2, gla_out_gain, diff_q_gain, diff_k_gain,
           lam_q1, lam_k1, lam_q2, lam_k2, diff_sub_gain, rel_bias, w_out, norm2_gain,
           w_router, b_router, w_gate_up, b_gate_up, w_down, b_down):
    stacked = dict(norm1_gain=norm1_gain, w_in=w_in, w_gla_gate2=w_gla_gate2,
                   b_gla_gate2=b_gla_gate2, gla_out_gain=gla_out_gain, diff_q_gain=diff_q_gain,
                   diff_k_gain=diff_k_gain, lam_q1=lam_q1, lam_k1=lam_k1, lam_q2=lam_q2,
                   lam_k2=lam_k2, diff_sub_gain=diff_sub_gain, w_out=w_out, norm2_gain=norm2_gain,
                   w_router=w_router, b_router=b_router, w_gate_up=w_gate_up,
                   b_gate_up=b_gate_up, w_down=w_down, b_down=b_down)
    h = x
    for l in range(norm1_gain.shape[0]):
        h = _layer(h, l, {k: v[l] for k, v in stacked.items()}, rel_bias)
    return h
```

```python
import functools
import math

import jax
import jax.numpy as jnp
from jax import lax
from jax.experimental import pallas as pl
from jax.experimental.pallas import tpu as pltpu

F32 = jnp.float32
BF16 = jnp.bfloat16
I32 = jnp.int32

EPS = 1e-6
GLA_HEADS = 4
GLA_DK = 64
GLA_DV = 128
GLA_RANK = 16
GLA_GATE_NORM = 16.0
GLA_CHUNK = 64
DIFF_HEADS = 4
DIFF_DH = 64
DIFF_DV = 128
REL_BUCKETS = 32
REL_MAX_DIST = 128
N_EXPERTS = 32
TOP_K = 4
SWIGLU_LIMIT = 7.0
SWIGLU_ALPHA = 1.702
MOE_BLOCK = 512
NEG_BIG = -1e30
LOG2E = math.log2(math.e)
ONES_ROWS = 16

VMEM_LIMIT = 56 * 1024 * 1024


def _cparams(sem, flags=None):
    return pltpu.CompilerParams(dimension_semantics=sem, vmem_limit_bytes=VMEM_LIMIT, flags=flags)


def _dot(a, b):
    return jnp.dot(a, b, preferred_element_type=F32)


def _dot_nt(a, b):
    return lax.dot_general(a, b, (((1,), (1,)), ((), ())), preferred_element_type=F32)


def _dot_tn(a, b):
    return lax.dot_general(a, b, (((0,), (0,)), ((), ())), preferred_element_type=F32)


def _split_bf16(x):
    hi = x.astype(BF16)
    lo = (x - hi.astype(F32)).astype(BF16)
    return hi, lo


def _proj_kernel(x_ref, g1_ref, wm_ref, wlow_ref, wg2_ref, bg2_ref, bd_ref, qg_ref, kg_ref,
                 gq_ref, gk_ref, gv_ref, gr_ref, la_ref, dq_ref, dk_ref, dvt_ref, wf_ref, *, tm, tv):
    @pl.when(pl.program_id(0) == 0)
    def _():
        wf_ref[...] = _dot(wlow_ref[...].astype(BF16), wg2_ref[...].astype(BF16)).astype(BF16)

    x = x_ref[...]
    ms = jnp.mean(x * x, axis=-1, keepdims=True)
    n = (x * lax.rsqrt(ms + EPS) * g1_ref[...]).astype(BF16)

    def mm(a, b):
        return _dot(n, wm_ref[:, a:b])

    gq_ref[...] = (mm(0, 256) * (GLA_DK ** -0.5)).astype(BF16)
    gk_ref[...] = mm(256, 512).astype(BF16)
    gv_ref[...] = mm(512, 1024).astype(BF16)
    gr_ref[...] = mm(1024, 1536).astype(BF16)

    z = _dot(n, wf_ref[...]) + bg2_ref[...]
    la_ref[...] = (jnp.minimum(z, 0.0) - jnp.log(1.0 + jnp.exp(-jnp.abs(z)))) * (1.0 / GLA_GATE_NORM)

    def qknorm(y, gain):
        ss = _dot((y * y).astype(BF16), bd_ref[...])
        return y * lax.rsqrt(ss * (1.0 / DIFF_DH) + EPS) * gain

    dq_ref[...] = (qknorm(mm(1536, 2048), qg_ref[...]) * (DIFF_DH ** -0.5 * LOG2E)).astype(BF16)
    dk_ref[...] = qknorm(mm(2048, 2560), kg_ref[...]).astype(BF16)
    dv = mm(2560, 3072)
    for u in range(tm // tv):
        dvt_ref[u] = dv[u * tv:(u + 1) * tv, :].T.astype(BF16)


def _proj(x2, g1, wm, wlow, wg2, bg2, bd, qg, kg, tm, tv):
    T, D = x2.shape
    const = lambda i: (0, 0)
    row = lambda i: (i, 0)
    outs = [(256, BF16), (256, BF16), (512, BF16), (512, BF16), (256, F32),
            (512, BF16), (512, BF16)]
    return pl.pallas_call(
        functools.partial(_proj_kernel, tm=tm, tv=tv),
        grid=(T // tm,),
        in_specs=[
            pl.BlockSpec((tm, D), row),
            pl.BlockSpec((1, D), const),
            pl.BlockSpec(wm.shape, const),
            pl.BlockSpec(wlow.shape, const),
            pl.BlockSpec(wg2.shape, const),
            pl.BlockSpec(bg2.shape, const),
            pl.BlockSpec(bd.shape, const),
            pl.BlockSpec(qg.shape, const),
            pl.BlockSpec(kg.shape, const),
        ],
        out_specs=[pl.BlockSpec((tm, w), row) for w, _ in outs]
        + [pl.BlockSpec((tm // tv, 512, tv), lambda i: (i, 0, 0))],
        out_shape=[jax.ShapeDtypeStruct((T, w), dt) for w, dt in outs]
        + [jax.ShapeDtypeStruct((T // tv, 512, tv), BF16)],
        scratch_shapes=[pltpu.VMEM((D, 256), BF16)],
        compiler_params=_cparams(("arbitrary",)),
        name="proj",
    )(x2, g1, wm, wlow, wg2, bg2, bd, qg, kg)


def _gla_kernel(q_ref, k_ref, v_ref, r_ref, la_ref, gain_ref, o_ref, state_ref, *, tg):
    C = GLA_CHUNK

    @pl.when(pl.program_id(1) == 0)
    def _():
        state_ref[...] = jnp.zeros(state_ref.shape, F32)

    ri = lax.broadcasted_iota(I32, (C, C), 0)
    ci = lax.broadcasted_iota(I32, (C, C), 1)
    causal = ri >= ci
    tril = jnp.where(causal, 1.0, 0.0).astype(BF16)
    ones_cv = jnp.ones((C, GLA_DV), BF16)
    gain = gain_ref[...]

    for c in range(tg // C):
        sl = slice(c * C, (c + 1) * C)
        la_hi, la_lo = _split_bf16(la_ref[sl, :])
        b = _dot(tril, la_hi) + _dot(tril, la_lo)
        b_last = b[C - 1:C, :]
        q = q_ref[sl, :].astype(F32)
        k = k_ref[sl, :].astype(F32)
        q_dec = (q * jnp.exp(b)).astype(BF16)
        k_dec = (k * jnp.exp(-b)).astype(BF16)
        k_last = (k * jnp.exp(b_last - b)).astype(BF16)
        for h in range(GLA_HEADS):
            hs = slice(h * GLA_DK, (h + 1) * GLA_DK)
            vs = slice(h * GLA_DV, (h + 1) * GLA_DV)
            v = v_ref[sl, vs]
            att = jnp.where(causal, _dot_nt(q_dec[:, hs], k_dec[:, hs]), 0.0)
            state = state_ref[h]
            o = _dot(att.astype(BF16), v) + _dot(q_dec[:, hs], state.astype(BF16))
            dcol = _dot_tn(la_hi[:, hs], ones_cv) + _dot_tn(la_lo[:, hs], ones_cv)
            state_ref[h] = state * jnp.exp(dcol) + _dot_tn(k_last[:, hs], v)
            on = o * lax.rsqrt(jnp.mean(o * o, axis=-1, keepdims=True) + EPS) * gain
            r = r_ref[sl, vs].astype(F32)
            o_ref[sl, vs] = (on * (r * jax.nn.sigmoid(r))).astype(o_ref.dtype)


def _gla(gq, gk, gv, gr, la, gain, B, S, tg):
    nt = S // tg
    row = lambda b, t: (b * nt + t, 0)
    const = lambda b, t: (0, 0)
    return pl.pallas_call(
        functools.partial(_gla_kernel, tg=tg),
        grid=(B, nt),
        in_specs=[
            pl.BlockSpec((tg, 256), row),
            pl.BlockSpec((tg, 256), row),
            pl.BlockSpec((tg, 512), row),
            pl.BlockSpec((tg, 512), row),
            pl.BlockSpec((tg, 256), row),
            pl.BlockSpec((1, GLA_DV), const),
        ],
        out_specs=pl.BlockSpec((tg, 512), row),
        out_shape=jax.ShapeDtypeStruct((B * S, 512), BF16),
        scratch_shapes=[pltpu.VMEM((GLA_HEADS, GLA_DK, GLA_DV), F32)],
        compiler_params=_cparams(("arbitrary", "arbitrary")),
        name="gla",
    )(gq, gk, gv, gr, la, gain)


def _diff_kernel(far_ref, q_ref, k_ref, vt_ref, bias_ref, lq1_ref, lk1_ref, lq2_ref, lk2_ref,
                 sg_ref, o_ref, qs_ref, m_ref, acc_ref, *, t, qc, lam_init):
    h = pl.program_id(1)
    i = pl.program_id(2)
    q = q_ref[...]
    lane = lax.broadcasted_iota(I32, q.shape, 1)
    zero = jnp.zeros_like(q)
    qs_ref[0:t, :] = jnp.where(lane < DIFF_DH, q, zero)
    qs_ref[t:2 * t, :] = jnp.where(lane >= DIFF_DH, q, zero)
    m_ref[...] = jnp.full(m_ref.shape, NEG_BIG, F32)
    acc_ref[...] = jnp.zeros(acc_ref.shape, F32)
    ones_rows = jnp.ones((ONES_ROWS, t), BF16)

    def step(j, tile_idx, shift):
        kj = k_ref[pl.ds(pl.multiple_of(j * t, t), t), :]
        vtj = jnp.concatenate([vt_ref[j], ones_rows], axis=0)
        nc = 2 * t // qc

        def scores(c):
            return _dot_nt(kj, qs_ref[c * qc:(c + 1) * qc, :])

        s_next = scores(0)
        for c in range(nc):
            cols = slice(c * qc, (c + 1) * qc)
            s = s_next
            if c + 1 < nc:
                s_next = scores(c + 1)
            if tile_idx is not None:
                c0 = (c * qc) % t
                s = s + bias_ref[tile_idx, :, c0:c0 + qc]
            m_prev = m_ref[:, cols]
            m_new = jnp.maximum(m_prev, jnp.max(s, axis=0, keepdims=True) + shift)
            alpha = jnp.exp2(m_prev - m_new)
            p = jnp.exp2((s - (m_new - shift)).astype(BF16))
            acc_ref[:, cols] = alpha * acc_ref[:, cols] + _dot(vtj, p)
            m_ref[:, cols] = m_new

    step(i, 0, 0.0)

    @pl.when(i > 0)
    def _():
        step(i - 1, 1, 0.0)

    far_bias = far_ref[h]

    def far(j, c):
        step(j, None, far_bias)
        return c

    lax.fori_loop(0, jnp.maximum(i - 1, 0), far, 0)

    lam = (jnp.exp(jnp.sum(lq1_ref[...] * lk1_ref[...], axis=-1, keepdims=True))
           - jnp.exp(jnp.sum(lq2_ref[...] * lk2_ref[...], axis=-1, keepdims=True)) + lam_init)
    out = acc_ref[0:DIFF_DV, :] / acc_ref[DIFF_DV:DIFF_DV + 1, :]
    o = out[:, :t] - lam * out[:, t:]
    on = o * lax.rsqrt(jnp.mean(o * o, axis=0, keepdims=True) + EPS) * sg_ref[...]
    o_ref[...] = (on * (1.0 - lam_init)).T.astype(o_ref.dtype)


def _diff(far_bias, dq, dk, dvt, bias_tiles, lq1, lk1, lq2, lk2, sg, B, S, t, lam_init):
    nq = S // t
    vec = lambda b, h, i: (0, 0)
    return pl.pallas_call(
        functools.partial(_diff_kernel, t=t, qc=min(t, 256), lam_init=lam_init),
        grid=(B, DIFF_HEADS, nq),
        in_specs=[
            pl.BlockSpec(memory_space=pltpu.SMEM),
            pl.BlockSpec((None, t, 128), lambda b, h, i: (b, i, h)),
            pl.BlockSpec((None, S, 128), lambda b, h, i: (b, 0, h)),
            pl.BlockSpec((nq, DIFF_DV, t), lambda b, h, i: (b, h, 0)),
            pl.BlockSpec((None, 2, t, t), lambda b, h, i: (h, 0, 0, 0)),
            pl.BlockSpec((1, DIFF_DH), vec),
            pl.BlockSpec((1, DIFF_DH), vec),
            pl.BlockSpec((1, DIFF_DH), vec),
            pl.BlockSpec((1, DIFF_DH), vec),
            pl.BlockSpec((DIFF_DV, 1), vec),
        ],
        out_specs=pl.BlockSpec((None, t, 128), lambda b, h, i: (b, i, h)),
        out_shape=jax.ShapeDtypeStruct((B, S, 512), BF16),
        scratch_shapes=[pltpu.VMEM((2 * t, 128), BF16), pltpu.VMEM((1, 2 * t), F32),
                        pltpu.VMEM((DIFF_DV + ONES_ROWS, 2 * t), F32)],
        compiler_params=_cparams(("arbitrary", "arbitrary", "arbitrary")),
        name="diff_attn",
    )(far_bias, dq, dk, dvt, bias_tiles, lq1, lk1, lq2, lk2, sg)


def _t5_causal_bucket(n):
    max_exact = REL_BUCKETS // 2
    nf = jnp.maximum(n, 1).astype(F32)
    large = max_exact + (jnp.log(nf / max_exact) / math.log(REL_MAX_DIST / max_exact)
                         * (REL_BUCKETS - max_exact)).astype(I32)
    large = jnp.minimum(large, REL_BUCKETS - 1)
    return jnp.where(n < max_exact, n, large)


def _bias_tiles(rel_bias, t):
    qi = jnp.arange(t)[:, None]
    kj = jnp.arange(t)[None, :]
    d0 = qi - kj
    diag = jnp.where((d0 >= 0)[..., None], rel_bias[_t5_causal_bucket(jnp.maximum(d0, 0))], NEG_BIG)
    sub = rel_bias[_t5_causal_bucket(d0 + t)]
    return jnp.stack([diag, sub], axis=0).transpose(3, 0, 2, 1).astype(F32)


def _mix_kernel(x_ref, yg_ref, yd_ref, wo_ref, g2_ref, wr_ref, br_ref,
                h_ref, m_ref, e_ref, gate_ref, rank_ref, cnt_ref, base_ref, *, tm):
    @pl.when(pl.program_id(0) == 0)
    def _():
        base_ref[...] = jnp.zeros(base_ref.shape, F32)

    h1 = x_ref[...] + _dot(yg_ref[...], wo_ref[0:512, :]) + _dot(yd_ref[...], wo_ref[512:1024, :])
    h_ref[...] = h1
    ms = jnp.mean(h1 * h1, axis=-1, keepdims=True)
    m = h1 * lax.rsqrt(ms + EPS) * g2_ref[...]
    m_ref[...] = m

    m_hi, m_lo = _split_bf16(m)
    w_hi, w_lo = _split_bf16(wr_ref[...])
    logits = _dot_nt(w_hi, m_hi) + _dot_nt(w_hi, m_lo) + _dot_nt(w_lo, m_hi) + br_ref[...]

    eidx = lax.broadcasted_iota(I32, logits.shape, 0)
    work = logits
    vals, idxs = [], []
    for _ in range(TOP_K):
        mx = jnp.max(work, axis=0, keepdims=True)
        ix = jnp.min(jnp.where(work == mx, eidx, N_EXPERTS), axis=0, keepdims=True)
        vals.append(mx)
        idxs.append(ix)
        work = jnp.where(eidx == ix, -jnp.inf, work)
    ex = [jnp.exp(v - vals[0]) for v in vals]
    den = ex[0] + ex[1] + ex[2] + ex[3]
    gate_ref[...] = jnp.concatenate([e / den for e in ex], axis=0)
    e_ref[...] = jnp.concatenate(idxs, axis=0)

    onehots = [jnp.where(eidx == ix, 1.0, 0.0) for ix in idxs]
    cnt = onehots[0] + onehots[1] + onehots[2] + onehots[3]
    ti = lax.broadcasted_iota(I32, (tm, tm), 0)
    tj = lax.broadcasted_iota(I32, (tm, tm), 1)
    upper = jnp.where(ti < tj, 1.0, 0.0).astype(BF16)
    prefix = _dot(cnt.astype(BF16), upper) + base_ref[...]
    rank_ref[...] = jnp.concatenate(
        [jnp.sum(oh * prefix, axis=0, keepdims=True) for oh in onehots], axis=0).astype(I32)
    base_ref[...] = base_ref[...] + jnp.sum(cnt, axis=1, keepdims=True)
    cnt_ref[...] = base_ref[...]


def _mix(x2, yg, yd, wo, g2, wr_t, br, tm):
    T, D = x2.shape
    row = lambda i: (i, 0)
    col = lambda i: (0, i)
    const = lambda i: (0, 0)
    return pl.pallas_call(
        functools.partial(_mix_kernel, tm=tm),
        grid=(T // tm,),
        in_specs=[
            pl.BlockSpec((tm, D), row),
            pl.BlockSpec((tm, 512), row),
            pl.BlockSpec((tm, 512), row),
            pl.BlockSpec(wo.shape, const),
            pl.BlockSpec((1, D), const),
            pl.BlockSpec(wr_t.shape, const),
            pl.BlockSpec(br.shape, const),
        ],
        out_specs=[
            pl.BlockSpec((tm, D), row),
            pl.BlockSpec((tm, D), row),
            pl.BlockSpec((TOP_K, tm), col),
            pl.BlockSpec((TOP_K, tm), col),
            pl.BlockSpec((TOP_K, tm), col),
            pl.BlockSpec((N_EXPERTS, 1), const),
        ],
        out_shape=[
            jax.ShapeDtypeStruct((T, D), F32),
            jax.ShapeDtypeStruct((T, D), F32),
            jax.ShapeDtypeStruct((TOP_K, T), I32),
            jax.ShapeDtypeStruct((TOP_K, T), F32),
            jax.ShapeDtypeStruct((TOP_K, T), I32),
            jax.ShapeDtypeStruct((N_EXPERTS, 1), F32),
        ],
        scratch_shapes=[pltpu.VMEM((N_EXPERTS, 1), F32)],
        compiler_params=_cparams(("arbitrary",)),
        name="mix_router",
    )(x2, yg, yd, wo, g2, wr_t, br)


def _row_copy_wait_all(src_ref, dst_ref, sem, n):
    def body(r, c):
        pltpu.make_async_copy(src_ref.at[pl.ds(0, 1)], dst_ref.at[pl.ds(0, 1)], sem).wait()
        return c
    lax.fori_loop(0, n, body, 0)


def _dispatch_kernel(dest_hbm, m_ref, xs_in, xs_hbm, dest_smem, sem_idx, sem, *, tm):
    del xs_in
    i = pl.program_id(0)
    cp = pltpu.make_async_copy(dest_hbm.at[i], dest_smem, sem_idx)
    cp.start()
    cp.wait()

    def body(r, c):
        for k in range(TOP_K):
            d = dest_smem[k * tm + r]
            pltpu.make_async_copy(m_ref.at[pl.ds(r, 1)], xs_hbm.at[pl.ds(d, 1)], sem).start()
        return c
    lax.fori_loop(0, tm, body, 0)
    _row_copy_wait_all(m_ref, xs_hbm, sem, TOP_K * tm)


def _dispatch(dest_tiles, m, xs_zero, tm):
    T, D = m.shape
    return pl.pallas_call(
        functools.partial(_dispatch_kernel, tm=tm),
        grid=(T // tm,),
        in_specs=[
            pl.BlockSpec(memory_space=pl.ANY),
            pl.BlockSpec((tm, D), lambda i: (i, 0)),
            pl.BlockSpec(memory_space=pl.ANY),
        ],
        out_specs=pl.BlockSpec(memory_space=pl.ANY),
        out_shape=jax.ShapeDtypeStruct(xs_zero.shape, xs_zero.dtype),
        scratch_shapes=[pltpu.SMEM((TOP_K * tm,), I32), pltpu.SemaphoreType.DMA,
                        pltpu.SemaphoreType.DMA],
        input_output_aliases={2: 0},
        compiler_params=_cparams(("arbitrary",)),
        name="moe_dispatch",
    )(dest_tiles, m, xs_zero)


def _experts_kernel(bexp_ref, nused_ref, xs_ref, wgu_ref, bgu_ref, wd_ref, bd_ref, ys_ref, *, dff):
    del bexp_ref

    @pl.when(pl.program_id(0) < nused_ref[0])
    def _():
        x = xs_ref[...].astype(BF16)
        gu = _dot(x, wgu_ref[...]) + bgu_ref[...]
        glu = jnp.minimum(gu[:, :dff], SWIGLU_LIMIT)
        lin = jnp.clip(gu[:, dff:], -SWIGLU_LIMIT, SWIGLU_LIMIT)
        hid = glu * jax.nn.sigmoid(SWIGLU_ALPHA * glu) * (lin + 1.0)
        ys_ref[...] = _dot(hid.astype(BF16), wd_ref[...]) + bd_ref[...]


def _experts(block_exp, n_used, xs, wgu, bgu, wd, bd):
    P, D = xs.shape
    dff = wd.shape[1]
    nb = P // MOE_BLOCK
    blk = lambda i, be, nu: (jnp.minimum(i, nu[0] - 1), 0)
    exp3 = lambda i, be, nu: (be[i], 0, 0)
    return pl.pallas_call(
        functools.partial(_experts_kernel, dff=dff),
        grid_spec=pltpu.PrefetchScalarGridSpec(
            num_scalar_prefetch=2,
            grid=(nb,),
            in_specs=[
                pl.BlockSpec((MOE_BLOCK, D), blk),
                pl.BlockSpec((None, D, 2 * dff), exp3),
                pl.BlockSpec((None, 1, 2 * dff), exp3),
                pl.BlockSpec((None, dff, D), exp3),
                pl.BlockSpec((None, 1, D), exp3),
            ],
            out_specs=pl.BlockSpec((MOE_BLOCK, D), blk),
        ),
        out_shape=jax.ShapeDtypeStruct((P, D), F32),
        compiler_params=_cparams(("arbitrary",)),
        name="moe_experts",
    )(block_exp, n_used, xs, wgu, bgu, wd, bd)


def _combine_kernel(dest_hbm, h_ref, gate_ref, ys_hbm, o_ref, buf, dest_smem, sem_idx, sem, *, tm):
    i = pl.program_id(0)
    cp = pltpu.make_async_copy(dest_hbm.at[i], dest_smem, sem_idx)
    cp.start()
    cp.wait()

    def body(r, c):
        for k in range(TOP_K):
            d = dest_smem[k * tm + r]
            pltpu.make_async_copy(ys_hbm.at[pl.ds(d, 1)], buf.at[k, pl.ds(r, 1)], sem).start()
        return c
    lax.fori_loop(0, tm, body, 0)
    _row_copy_wait_all(ys_hbm, buf.at[0], sem, TOP_K * tm)

    g = gate_ref[...]
    acc = h_ref[...]
    for k in range(TOP_K):
        acc = acc + g[:, k:k + 1] * buf[k]
    o_ref[...] = acc


def _combine(dest_tiles, h1, gates_t, ys, tm):
    T, D = h1.shape
    return pl.pallas_call(
        functools.partial(_combine_kernel, tm=tm),
        grid=(T // tm,),
        in_specs=[
            pl.BlockSpec(memory_space=pl.ANY),
            pl.BlockSpec((tm, D), lambda i: (i, 0)),
            pl.BlockSpec((tm, TOP_K), lambda i: (i, 0)),
            pl.BlockSpec(memory_space=pl.ANY),
        ],
        out_specs=pl.BlockSpec((tm, D), lambda i: (i, 0)),
        out_shape=jax.ShapeDtypeStruct((T, D), F32),
        scratch_shapes=[pltpu.VMEM((TOP_K, tm, D), F32), pltpu.SMEM((TOP_K * tm,), I32),
                        pltpu.SemaphoreType.DMA, pltpu.SemaphoreType.DMA],
        compiler_params=_cparams(("arbitrary",)),
        name="moe_combine",
    )(dest_tiles, h1, gates_t, ys)


def _tile(n, pref):
    t = min(n, pref)
    assert n % t == 0, (n, pref)
    return t


def _layer(h, l, p, rel_bias):
    B, S, D = h.shape
    T = B * S
    x2 = h.reshape(T, D)
    lam_init = 0.8 - 0.6 * math.exp(-0.3 * l)

    w_in = p['w_in']
    c_low = 1024
    wm = jnp.concatenate([w_in[:, :c_low], w_in[:, c_low + GLA_RANK:]], axis=1).astype(BF16)
    wlow = jnp.pad(w_in[:, c_low:c_low + GLA_RANK], ((0, 0), (0, 128 - GLA_RANK)))
    wg2 = jnp.pad(p['w_gla_gate2'], ((0, 128 - GLA_RANK), (0, 0)))
    bg2 = p['b_gla_gate2'].reshape(1, -1)
    gidx = jnp.arange(512) // DIFF_DH
    bd = (gidx[:, None] == gidx[None, :]).astype(BF16)
    qg = jnp.tile(p['diff_q_gain'], 512 // DIFF_DH).reshape(1, 512)
    kg = jnp.tile(p['diff_k_gain'], 512 // DIFF_DH).reshape(1, 512)

    tm = _tile(T, 512)
    t = _tile(S, 512)
    assert t >= 128
    gq, gk, gv, gr, la, dq, dk, dvt = _proj(
        x2, p['norm1_gain'].reshape(1, D), wm, wlow, wg2, bg2, bd, qg, kg, tm, t)

    y_gla = _gla(gq, gk, gv, gr, la, p['gla_out_gain'].reshape(1, GLA_DV), B, S, _tile(S, 256))

    rel_bias2 = rel_bias.astype(F32) * LOG2E
    y_diff = _diff(rel_bias2[REL_BUCKETS - 1],
                   dq.reshape(B, S, 512), dk.reshape(B, S, 512), dvt,
                   _bias_tiles(rel_bias2, t),
                   p['lam_q1'].reshape(1, -1), p['lam_k1'].reshape(1, -1),
                   p['lam_q2'].reshape(1, -1), p['lam_k2'].reshape(1, -1),
                   p['diff_sub_gain'].reshape(-1, 1), B, S, t, lam_init).reshape(T, 512)

    h1, m, e_t, gate_t, rank_t, counts = _mix(
        x2, y_gla, y_diff, p['w_out'].astype(BF16), p['norm2_gain'].reshape(1, D),
        p['w_router'].T, p['b_router'].reshape(-1, 1), tm)

    counts = counts.reshape(-1).astype(I32)
    padded = (counts + MOE_BLOCK - 1) // MOE_BLOCK * MOE_BLOCK
    pad_end = jnp.cumsum(padded)
    pad_start = pad_end - padded
    n_blocks = -(-(T * TOP_K) // MOE_BLOCK) + N_EXPERTS
    P = n_blocks * MOE_BLOCK
    blk_start = jnp.arange(n_blocks, dtype=I32) * MOE_BLOCK
    block_exp = jnp.minimum(jnp.sum((pad_end[None, :] <= blk_start[:, None]).astype(I32), axis=1),
                            N_EXPERTS - 1)
    n_used = (pad_end[-1:] // MOE_BLOCK).astype(I32)
    eids = jnp.arange(N_EXPERTS, dtype=I32)
    dest = rank_t + jnp.sum(jnp.where(e_t[..., None] == eids, pad_start, 0), axis=-1)

    tmd = _tile(T, 256)
    dest_tiles = dest.reshape(TOP_K, T // tmd, tmd).transpose(1, 0, 2).reshape(T // tmd, TOP_K * tmd)

    xs = _dispatch(dest_tiles, m, jnp.zeros((P, D), F32), tmd)
    ys = _experts(block_exp, n_used, xs, p['w_gate_up'].astype(BF16),
                  p['b_gate_up'].reshape(N_EXPERTS, 1, -1), p['w_down'].astype(BF16),
                  p['b_down'].reshape(N_EXPERTS, 1, -1))
    out = _combine(dest_tiles, h1, gate_t.T, ys, tmd)
    return out.reshape(B, S, D)


def kernel(x, norm1_gain, w_in, w_gla_gate2, b_gla_gate2, gla_out_gain, diff_q_gain, diff_k_gain,
           lam_q1, lam_k1, lam_q2, lam_k2, diff_sub_gain, rel_bias, w_out, norm2_gain,
           w_router, b_router, w_gate_up, b_gate_up, w_down, b_down):
    stacked = dict(norm1_gain=norm1_gain, w_in=w_in, w_gla_gate2=w_gla_gate2,
                   b_gla_gate2=b_gla_gate2, gla_out_gain=gla_out_gain, diff_q_gain=diff_q_gain,
                   diff_k_gain=diff_k_gain, lam_q1=lam_q1, lam_k1=lam_k1, lam_q2=lam_q2,
                   lam_k2=lam_k2, diff_sub_gain=diff_sub_gain, w_out=w_out, norm2_gain=norm2_gain,
                   w_router=w_router, b_router=b_router, w_gate_up=w_gate_up,
                   b_gate_up=b_gate_up, w_down=w_down, b_down=b_down)
    h = x
    for l in range(norm1_gain.shape[0]):
        h = _layer(h, l, {k: v[l] for k, v in stacked.items()}, rel_bias)
    return h
```

```python
import functools
import math

import jax
import jax.numpy as jnp
from jax import lax
from jax.experimental import pallas as pl
from jax.experimental.pallas import tpu as pltpu

F32 = jnp.float32
BF16 = jnp.bfloat16
I32 = jnp.int32

EPS = 1e-6
GLA_HEADS = 4
GLA_DK = 64
GLA_DV = 128
GLA_RANK = 16
GLA_GATE_NORM = 16.0
GLA_CHUNK = 64
DIFF_HEADS = 4
DIFF_DH = 64
DIFF_DV = 128
REL_BUCKETS = 32
REL_MAX_DIST = 128
N_EXPERTS = 32
TOP_K = 4
SWIGLU_LIMIT = 7.0
SWIGLU_ALPHA = 1.702
MOE_BLOCK = 512
NEG_BIG = -1e30
LOG2E = math.log2(math.e)
ONES_ROWS = 16

VMEM_LIMIT = 56 * 1024 * 1024


def _cparams(sem, flags=None):
    return pltpu.CompilerParams(dimension_semantics=sem, vmem_limit_bytes=VMEM_LIMIT, flags=flags)


def _dot(a, b):
    return jnp.dot(a, b, preferred_element_type=F32)


def _dot_nt(a, b):
    return lax.dot_general(a, b, (((1,), (1,)), ((), ())), preferred_element_type=F32)


def _dot_tn(a, b):
    return lax.dot_general(a, b, (((0,), (0,)), ((), ())), preferred_element_type=F32)


def _split_bf16(x):
    hi = x.astype(BF16)
    lo = (x - hi.astype(F32)).astype(BF16)
    return hi, lo


SLAB = 8


def _store_slabs(ref, x):
    n = x.shape[0]
    for s in range(SLAB):
        ref[pl.ds(s, n, stride=SLAB), :] = x[:, s * 128:(s + 1) * 128]


def _load_slabs(ref, n):
    return jnp.concatenate([ref[pl.ds(s, n, stride=SLAB), :] for s in range(SLAB)], axis=1)


def _proj_kernel(x_ref, g1_ref, wm_ref, wlow_ref, wg2_ref, bg2_ref, bd_ref, qg_ref, kg_ref,
                 gq_ref, gk_ref, gv_ref, gr_ref, la_ref, dq_ref, dk_ref, dvt_ref, wf_ref, *, tm, tv):
    @pl.when(pl.program_id(0) == 0)
    def _():
        wf_ref[...] = _dot(wlow_ref[...].astype(BF16), wg2_ref[...].astype(BF16)).astype(BF16)

    x = x_ref[...]
    ms = jnp.mean(x * x, axis=-1, keepdims=True)
    n = (x * lax.rsqrt(ms + EPS) * g1_ref[...]).astype(BF16)

    def mm(a, b):
        return _dot(n, wm_ref[:, a:b])

    gq_ref[...] = (mm(0, 256) * (GLA_DK ** -0.5)).astype(BF16)
    gk_ref[...] = mm(256, 512).astype(BF16)
    gv_ref[...] = mm(512, 1024).astype(BF16)
    gr_ref[...] = mm(1024, 1536).astype(BF16)

    z = _dot(n, wf_ref[...]) + bg2_ref[...]
    la_ref[...] = (jnp.minimum(z, 0.0) - jnp.log(1.0 + jnp.exp(-jnp.abs(z)))) * (1.0 / GLA_GATE_NORM)

    def qknorm(y, gain):
        ss = _dot((y * y).astype(BF16), bd_ref[...])
        return y * lax.rsqrt(ss * (1.0 / DIFF_DH) + EPS) * gain

    dq_ref[...] = (qknorm(mm(1536, 2048), qg_ref[...]) * (DIFF_DH ** -0.5 * LOG2E)).astype(BF16)
    dk_ref[...] = qknorm(mm(2048, 2560), kg_ref[...]).astype(BF16)
    dv = mm(2560, 3072)
    for u in range(tm // tv):
        dvt_ref[u] = dv[u * tv:(u + 1) * tv, :].T.astype(BF16)


def _proj(x2, g1, wm, wlow, wg2, bg2, bd, qg, kg, tm, tv):
    T, D = x2.shape
    const = lambda i: (0, 0)
    row = lambda i: (i, 0)
    outs = [(256, BF16), (256, BF16), (512, BF16), (512, BF16), (256, F32),
            (512, BF16), (512, BF16)]
    return pl.pallas_call(
        functools.partial(_proj_kernel, tm=tm, tv=tv),
        grid=(T // tm,),
        in_specs=[
            pl.BlockSpec((tm, D), row),
            pl.BlockSpec((1, D), const),
            pl.BlockSpec(wm.shape, const),
            pl.BlockSpec(wlow.shape, const),
            pl.BlockSpec(wg2.shape, const),
            pl.BlockSpec(bg2.shape, const),
            pl.BlockSpec(bd.shape, const),
            pl.BlockSpec(qg.shape, const),
            pl.BlockSpec(kg.shape, const),
        ],
        out_specs=[pl.BlockSpec((tm, w), row) for w, _ in outs]
        + [pl.BlockSpec((tm // tv, 512, tv), lambda i: (i, 0, 0))],
        out_shape=[jax.ShapeDtypeStruct((T, w), dt) for w, dt in outs]
        + [jax.ShapeDtypeStruct((T // tv, 512, tv), BF16)],
        scratch_shapes=[pltpu.VMEM((D, 256), BF16)],
        compiler_params=_cparams(("arbitrary",)),
        name="proj",
    )(x2, g1, wm, wlow, wg2, bg2, bd, qg, kg)


def _gla_kernel(q_ref, k_ref, v_ref, r_ref, la_ref, gain_ref, o_ref, state_ref, *, tg):
    C = GLA_CHUNK

    @pl.when(pl.program_id(1) == 0)
    def _():
        state_ref[...] = jnp.zeros(state_ref.shape, F32)

    ri = lax.broadcasted_iota(I32, (C, C), 0)
    ci = lax.broadcasted_iota(I32, (C, C), 1)
    causal = ri >= ci
    tril = jnp.where(causal, 1.0, 0.0).astype(BF16)
    ones_cv = jnp.ones((C, GLA_DV), BF16)
    gain = gain_ref[...]

    for c in range(tg // C):
        sl = slice(c * C, (c + 1) * C)
        la_hi, la_lo = _split_bf16(la_ref[sl, :])
        b = _dot(tril, la_hi) + _dot(tril, la_lo)
        b_last = b[C - 1:C, :]
        q = q_ref[sl, :].astype(F32)
        k = k_ref[sl, :].astype(F32)
        q_dec = (q * jnp.exp(b)).astype(BF16)
        k_dec = (k * jnp.exp(-b)).astype(BF16)
        k_last = (k * jnp.exp(b_last - b)).astype(BF16)
        for h in range(GLA_HEADS):
            hs = slice(h * GLA_DK, (h + 1) * GLA_DK)
            vs = slice(h * GLA_DV, (h + 1) * GLA_DV)
            v = v_ref[sl, vs]
            att = jnp.where(causal, _dot_nt(q_dec[:, hs], k_dec[:, hs]), 0.0)
            state = state_ref[h]
            o = _dot(att.astype(BF16), v) + _dot(q_dec[:, hs], state.astype(BF16))
            dcol = _dot_tn(la_hi[:, hs], ones_cv) + _dot_tn(la_lo[:, hs], ones_cv)
            state_ref[h] = state * jnp.exp(dcol) + _dot_tn(k_last[:, hs], v)
            on = o * lax.rsqrt(jnp.mean(o * o, axis=-1, keepdims=True) + EPS) * gain
            r = r_ref[sl, vs].astype(F32)
            o_ref[sl, vs] = (on * (r * jax.nn.sigmoid(r))).astype(o_ref.dtype)


def _gla(gq, gk, gv, gr, la, gain, B, S, tg):
    nt = S // tg
    row = lambda b, t: (b * nt + t, 0)
    const = lambda b, t: (0, 0)
    return pl.pallas_call(
        functools.partial(_gla_kernel, tg=tg),
        grid=(B, nt),
        in_specs=[
            pl.BlockSpec((tg, 256), row),
            pl.BlockSpec((tg, 256), row),
            pl.BlockSpec((tg, 512), row),
            pl.BlockSpec((tg, 512), row),
            pl.BlockSpec((tg, 256), row),
            pl.BlockSpec((1, GLA_DV), const),
        ],
        out_specs=pl.BlockSpec((tg, 512), row),
        out_shape=jax.ShapeDtypeStruct((B * S, 512), BF16),
        scratch_shapes=[pltpu.VMEM((GLA_HEADS, GLA_DK, GLA_DV), F32)],
        compiler_params=_cparams(("arbitrary", "arbitrary")),
        name="gla",
    )(gq, gk, gv, gr, la, gain)


def _diff_kernel(far_ref, q_ref, k_ref, vt_ref, bias_ref, lq1_ref, lk1_ref, lq2_ref, lk2_ref,
                 sg_ref, o_ref, qs_ref, m_ref, acc_ref, *, t, qc, lam_init):
    h = pl.program_id(1)
    i = pl.program_id(2)
    q = q_ref[...]
    lane = lax.broadcasted_iota(I32, q.shape, 1)
    zero = jnp.zeros_like(q)
    qs_ref[0:t, :] = jnp.where(lane < DIFF_DH, q, zero)
    qs_ref[t:2 * t, :] = jnp.where(lane >= DIFF_DH, q, zero)
    m_ref[...] = jnp.full(m_ref.shape, NEG_BIG, F32)
    acc_ref[...] = jnp.zeros(acc_ref.shape, F32)
    ones_rows = jnp.ones((ONES_ROWS, t), BF16)

    def step(j, tile_idx, shift):
        kj = k_ref[pl.ds(pl.multiple_of(j * t, t), t), :]
        vtj = jnp.concatenate([vt_ref[j], ones_rows], axis=0)
        nc = 2 * t // qc

        def scores(c):
            return _dot_nt(kj, qs_ref[c * qc:(c + 1) * qc, :])

        s_next = scores(0)
        for c in range(nc):
            cols = slice(c * qc, (c + 1) * qc)
            s = s_next
            if c + 1 < nc:
                s_next = scores(c + 1)
            if tile_idx is not None:
                c0 = (c * qc) % t
                s = s + bias_ref[tile_idx, :, c0:c0 + qc]
            m_prev = m_ref[:, cols]
            m_new = jnp.maximum(m_prev, jnp.max(s, axis=0, keepdims=True) + shift)
            alpha = jnp.exp2(m_prev - m_new)
            p = jnp.exp2((s - (m_new - shift)).astype(BF16))
            acc_ref[:, cols] = alpha * acc_ref[:, cols] + _dot(vtj, p)
            m_ref[:, cols] = m_new

    step(i, 0, 0.0)

    @pl.when(i > 0)
    def _():
        step(i - 1, 1, 0.0)

    far_bias = far_ref[h]

    def far(j, c):
        step(j, None, far_bias)
        return c

    lax.fori_loop(0, jnp.maximum(i - 1, 0), far, 0)

    lam = (jnp.exp(jnp.sum(lq1_ref[...] * lk1_ref[...], axis=-1, keepdims=True))
           - jnp.exp(jnp.sum(lq2_ref[...] * lk2_ref[...], axis=-1, keepdims=True)) + lam_init)
    out = acc_ref[0:DIFF_DV, :] / acc_ref[DIFF_DV:DIFF_DV + 1, :]
    o = out[:, :t] - lam * out[:, t:]
    on = o * lax.rsqrt(jnp.mean(o * o, axis=0, keepdims=True) + EPS) * sg_ref[...]
    o_ref[...] = (on * (1.0 - lam_init)).T.astype(o_ref.dtype)


def _diff(far_bias, dq, dk, dvt, bias_tiles, lq1, lk1, lq2, lk2, sg, B, S, t, lam_init):
    nq = S // t
    vec = lambda b, h, i: (0, 0)
    return pl.pallas_call(
        functools.partial(_diff_kernel, t=t, qc=min(t, 256), lam_init=lam_init),
        grid=(B, DIFF_HEADS, nq),
        in_specs=[
            pl.BlockSpec(memory_space=pltpu.SMEM),
            pl.BlockSpec((None, t, 128), lambda b, h, i: (b, i, h)),
            pl.BlockSpec((None, S, 128), lambda b, h, i: (b, 0, h)),
            pl.BlockSpec((nq, DIFF_DV, t), lambda b, h, i: (b, h, 0)),
            pl.BlockSpec((None, 2, t, t), lambda b, h, i: (h, 0, 0, 0)),
            pl.BlockSpec((1, DIFF_DH), vec),
            pl.BlockSpec((1, DIFF_DH), vec),
            pl.BlockSpec((1, DIFF_DH), vec),
            pl.BlockSpec((1, DIFF_DH), vec),
            pl.BlockSpec((DIFF_DV, 1), vec),
        ],
        out_specs=pl.BlockSpec((None, t, 128), lambda b, h, i: (b, i, h)),
        out_shape=jax.ShapeDtypeStruct((B, S, 512), BF16),
        scratch_shapes=[pltpu.VMEM((2 * t, 128), BF16), pltpu.VMEM((1, 2 * t), F32),
                        pltpu.VMEM((DIFF_DV + ONES_ROWS, 2 * t), F32)],
        compiler_params=_cparams(("arbitrary", "arbitrary", "arbitrary")),
        name="diff_attn",
    )(far_bias, dq, dk, dvt, bias_tiles, lq1, lk1, lq2, lk2, sg)


def _t5_causal_bucket(n):
    max_exact = REL_BUCKETS // 2
    nf = jnp.maximum(n, 1).astype(F32)
    large = max_exact + (jnp.log(nf / max_exact) / math.log(REL_MAX_DIST / max_exact)
                         * (REL_BUCKETS - max_exact)).astype(I32)
    large = jnp.minimum(large, REL_BUCKETS - 1)
    return jnp.where(n < max_exact, n, large)


def _bias_tiles(rel_bias, t):
    kj = jnp.arange(t)[:, None]
    qi = jnp.arange(t)[None, :]
    d0 = qi - kj

    def lookup(bucket):
        hot = bucket[None, :, :, None] == jnp.arange(REL_BUCKETS)
        return jnp.sum(jnp.where(hot, rel_bias.T[:, None, None, :], 0.0), axis=-1)

    diag = jnp.where(d0 >= 0, lookup(_t5_causal_bucket(jnp.maximum(d0, 0))), NEG_BIG)
    sub = lookup(_t5_causal_bucket(d0 + t))
    return jnp.stack([diag, sub], axis=1).astype(F32)


def _mix_kernel(x_ref, yg_ref, yd_ref, wo_ref, g2_ref, wr_ref, br_ref,
                h_ref, m_ref, e_ref, gate_ref, rank_ref, cnt_ref, base_ref, *, tm):
    @pl.when(pl.program_id(0) == 0)
    def _():
        base_ref[...] = jnp.zeros(base_ref.shape, F32)

    h1 = x_ref[...] + _dot(yg_ref[...], wo_ref[0:512, :]) + _dot(yd_ref[...], wo_ref[512:1024, :])
    h_ref[...] = h1
    ms = jnp.mean(h1 * h1, axis=-1, keepdims=True)
    m = h1 * lax.rsqrt(ms + EPS) * g2_ref[...]
    _store_slabs(m_ref, m)

    m_hi, m_lo = _split_bf16(m)
    w_hi, w_lo = _split_bf16(wr_ref[...])
    logits = _dot_nt(w_hi, m_hi) + _dot_nt(w_hi, m_lo) + _dot_nt(w_lo, m_hi) + br_ref[...]

    eidx = lax.broadcasted_iota(I32, logits.shape, 0)
    work = logits
    vals, idxs = [], []
    for _ in range(TOP_K):
        mx = jnp.max(work, axis=0, keepdims=True)
        ix = jnp.min(jnp.where(work == mx, eidx, N_EXPERTS), axis=0, keepdims=True)
        vals.append(mx)
        idxs.append(ix)
        work = jnp.where(eidx == ix, -jnp.inf, work)
    ex = [jnp.exp(v - vals[0]) for v in vals]
    den = ex[0] + ex[1] + ex[2] + ex[3]
    gate_ref[...] = jnp.concatenate([e / den for e in ex], axis=0)
    e_ref[...] = jnp.concatenate(idxs, axis=0)

    onehots = [jnp.where(eidx == ix, 1.0, 0.0) for ix in idxs]
    cnt = onehots[0] + onehots[1] + onehots[2] + onehots[3]
    ti = lax.broadcasted_iota(I32, (tm, tm), 0)
    tj = lax.broadcasted_iota(I32, (tm, tm), 1)
    upper = jnp.where(ti < tj, 1.0, 0.0).astype(BF16)
    prefix = _dot(cnt.astype(BF16), upper) + base_ref[...]
    rank_ref[...] = jnp.concatenate(
        [jnp.sum(oh * prefix, axis=0, keepdims=True) for oh in onehots], axis=0).astype(I32)
    base_ref[...] = base_ref[...] + jnp.sum(cnt, axis=1, keepdims=True)
    cnt_ref[...] = base_ref[...]


def _mix(x2, yg, yd, wo, g2, wr_t, br, tm):
    T, D = x2.shape
    row = lambda i: (i, 0)
    col = lambda i: (0, i)
    const = lambda i: (0, 0)
    return pl.pallas_call(
        functools.partial(_mix_kernel, tm=tm),
        grid=(T // tm,),
        in_specs=[
            pl.BlockSpec((tm, D), row),
            pl.BlockSpec((tm, 512), row),
            pl.BlockSpec((tm, 512), row),
            pl.BlockSpec(wo.shape, const),
            pl.BlockSpec((1, D), const),
            pl.BlockSpec(wr_t.shape, const),
            pl.BlockSpec(br.shape, const),
        ],
        out_specs=[
            pl.BlockSpec((tm, D), row),
            pl.BlockSpec((tm * SLAB, 128), row),
            pl.BlockSpec((TOP_K, tm), col),
            pl.BlockSpec((TOP_K, tm), col),
            pl.BlockSpec((TOP_K, tm), col),
            pl.BlockSpec((N_EXPERTS, 1), const),
        ],
        out_shape=[
            jax.ShapeDtypeStruct((T, D), F32),
            jax.ShapeDtypeStruct((T * SLAB, 128), F32),
            jax.ShapeDtypeStruct((TOP_K, T), I32),
            jax.ShapeDtypeStruct((TOP_K, T), F32),
            jax.ShapeDtypeStruct((TOP_K, T), I32),
            jax.ShapeDtypeStruct((N_EXPERTS, 1), F32),
        ],
        scratch_shapes=[pltpu.VMEM((N_EXPERTS, 1), F32)],
        compiler_params=_cparams(("arbitrary",)),
        name="mix_router",
    )(x2, yg, yd, wo, g2, wr_t, br)


def _slab(ref, row):
    return ref.at[pl.ds(pl.multiple_of(row * SLAB, SLAB), SLAB)]


def _slab_wait_all(src_ref, dst_ref, sem, n):
    def body(r, c):
        pltpu.make_async_copy(_slab(src_ref, 0), _slab(dst_ref, 0), sem).wait()
        return c
    lax.fori_loop(0, n, body, 0, unroll=8)


def _dispatch_kernel(pend_ref, padded_ref, dest_hbm, m_ref, xs_hbm, dest_smem, zero_ref,
                     sem_idx, sem_zero, sem, *, tm):
    i = pl.program_id(0)

    @pl.when(i == 0)
    def _():
        zero_ref[...] = jnp.zeros(zero_ref.shape, F32)

        def last_block(e):
            return xs_hbm.at[pl.ds(pl.multiple_of((pend_ref[e] - MOE_BLOCK) * SLAB, SLAB),
                                   MOE_BLOCK * SLAB)]

        def start(e, c):
            @pl.when(padded_ref[e] > 0)
            def _():
                pltpu.make_async_copy(zero_ref, last_block(e), sem_zero).start()
            return c

        def wait(e, c):
            @pl.when(padded_ref[e] > 0)
            def _():
                pltpu.make_async_copy(zero_ref, last_block(e), sem_zero).wait()
            return c

        lax.fori_loop(0, N_EXPERTS, start, 0)
        lax.fori_loop(0, N_EXPERTS, wait, 0)

    cp = pltpu.make_async_copy(dest_hbm.at[i], dest_smem, sem_idx)
    cp.start()
    cp.wait()

    def body(r, c):
        for k in range(TOP_K):
            d = dest_smem[k * tm + r]
            pltpu.make_async_copy(_slab(m_ref, r), _slab(xs_hbm, d), sem).start()
        return c
    lax.fori_loop(0, tm, body, 0, unroll=4)
    _slab_wait_all(m_ref, xs_hbm, sem, TOP_K * tm)


def _dispatch(pad_end, padded, dest_tiles, m_slabs, P, tm):
    T = m_slabs.shape[0] // SLAB
    return pl.pallas_call(
        functools.partial(_dispatch_kernel, tm=tm),
        grid_spec=pltpu.PrefetchScalarGridSpec(
            num_scalar_prefetch=2,
            grid=(T // tm,),
            in_specs=[
                pl.BlockSpec(memory_space=pl.ANY),
                pl.BlockSpec((tm * SLAB, 128), lambda i, pe, pd: (i, 0)),
            ],
            out_specs=pl.BlockSpec(memory_space=pl.ANY),
            scratch_shapes=[pltpu.SMEM((TOP_K * tm,), I32),
                            pltpu.VMEM((MOE_BLOCK * SLAB, 128), F32),
                            pltpu.SemaphoreType.DMA, pltpu.SemaphoreType.DMA,
                            pltpu.SemaphoreType.DMA],
        ),
        out_shape=jax.ShapeDtypeStruct((P * SLAB, 128), F32),
        compiler_params=_cparams(("arbitrary",)),
        name="moe_dispatch",
    )(pad_end, padded, dest_tiles, m_slabs)


def _experts_kernel(bexp_ref, nused_ref, xs_ref, wgu_ref, bgu_ref, wd_ref, bd_ref, ys_ref, *, dff):
    del bexp_ref

    @pl.when(pl.program_id(0) < nused_ref[0])
    def _():
        x = _load_slabs(xs_ref, MOE_BLOCK).astype(BF16)
        gu = _dot(x, wgu_ref[...]) + bgu_ref[...]
        glu = jnp.minimum(gu[:, :dff], SWIGLU_LIMIT)
        lin = jnp.clip(gu[:, dff:], -SWIGLU_LIMIT, SWIGLU_LIMIT)
        hid = glu * jax.nn.sigmoid(SWIGLU_ALPHA * glu) * (lin + 1.0)
        _store_slabs(ys_ref, _dot(hid.astype(BF16), wd_ref[...]) + bd_ref[...])


def _experts(block_exp, n_used, xs, wgu, bgu, wd, bd):
    D, dff = wd.shape[2], wd.shape[1]
    assert D == SLAB * 128
    P = xs.shape[0] // SLAB
    nb = P // MOE_BLOCK
    blk = lambda i, be, nu: (jnp.minimum(i, nu[0] - 1), 0)
    exp3 = lambda i, be, nu: (be[i], 0, 0)
    return pl.pallas_call(
        functools.partial(_experts_kernel, dff=dff),
        grid_spec=pltpu.PrefetchScalarGridSpec(
            num_scalar_prefetch=2,
            grid=(nb,),
            in_specs=[
                pl.BlockSpec((MOE_BLOCK * SLAB, 128), blk),
                pl.BlockSpec((None, D, 2 * dff), exp3),
                pl.BlockSpec((None, 1, 2 * dff), exp3),
                pl.BlockSpec((None, dff, D), exp3),
                pl.BlockSpec((None, 1, D), exp3),
            ],
            out_specs=pl.BlockSpec((MOE_BLOCK * SLAB, 128), blk),
        ),
        out_shape=jax.ShapeDtypeStruct((P * SLAB, 128), F32),
        compiler_params=_cparams(("arbitrary",)),
        name="moe_experts",
    )(block_exp, n_used, xs, wgu, bgu, wd, bd)


def _combine_kernel(dest_hbm, h_ref, gate_ref, ys_hbm, o_ref, buf, dest_smem, sem_idx, sem,
                    *, tm, nt):
    i = pl.program_id(0)
    slot = lax.rem(i, 2)

    def fetch(tile, slot):
        cp = pltpu.make_async_copy(dest_hbm.at[tile], dest_smem, sem_idx)
        cp.start()
        cp.wait()

        def body(r, c):
            for k in range(TOP_K):
                d = dest_smem[k * tm + r]
                pltpu.make_async_copy(_slab(ys_hbm, d), _slab(buf.at[slot, k], r),
                                      sem.at[slot]).start()
            return c
        lax.fori_loop(0, tm, body, 0, unroll=4)

    @pl.when(i == 0)
    def _():
        fetch(0, 0)

    @pl.when(i + 1 < nt)
    def _():
        fetch(i + 1, 1 - slot)

    _slab_wait_all(ys_hbm, buf.at[slot, 0], sem.at[slot], TOP_K * tm)

    g = gate_ref[...]
    acc = h_ref[...]
    for k in range(TOP_K):
        acc = acc + g[:, k:k + 1] * _load_slabs(buf.at[slot, k], tm)
    o_ref[...] = acc


def _combine(dest_tiles, h1, gates_t, ys, tm):
    T, D = h1.shape
    nt = T // tm
    return pl.pallas_call(
        functools.partial(_combine_kernel, tm=tm, nt=nt),
        grid=(nt,),
        in_specs=[
            pl.BlockSpec(memory_space=pl.ANY),
            pl.BlockSpec((tm, D), lambda i: (i, 0)),
            pl.BlockSpec((tm, TOP_K), lambda i: (i, 0)),
            pl.BlockSpec(memory_space=pl.ANY),
        ],
        out_specs=pl.BlockSpec((tm, D), lambda i: (i, 0)),
        out_shape=jax.ShapeDtypeStruct((T, D), F32),
        scratch_shapes=[pltpu.VMEM((2, TOP_K, tm * SLAB, 128), F32),
                        pltpu.SMEM((TOP_K * tm,), I32),
                        pltpu.SemaphoreType.DMA, pltpu.SemaphoreType.DMA((2,))],
        compiler_params=_cparams(("arbitrary",)),
        name="moe_combine",
    )(dest_tiles, h1, gates_t, ys)


def _tile(n, pref):
    t = min(n, pref)
    assert n % t == 0, (n, pref)
    return t


def _layer(h, l, p, rel_bias):
    B, S, D = h.shape
    T = B * S
    x2 = h.reshape(T, D)
    lam_init = 0.8 - 0.6 * math.exp(-0.3 * l)

    w_in = p['w_in']
    c_low = 1024
    wm = jnp.concatenate([w_in[:, :c_low], w_in[:, c_low + GLA_RANK:]], axis=1).astype(BF16)
    wlow = jnp.pad(w_in[:, c_low:c_low + GLA_RANK], ((0, 0), (0, 128 - GLA_RANK)))
    wg2 = jnp.pad(p['w_gla_gate2'], ((0, 128 - GLA_RANK), (0, 0)))
    bg2 = p['b_gla_gate2'].reshape(1, -1)
    gidx = jnp.arange(512) // DIFF_DH
    bd = (gidx[:, None] == gidx[None, :]).astype(BF16)
    qg = jnp.tile(p['diff_q_gain'], 512 // DIFF_DH).reshape(1, 512)
    kg = jnp.tile(p['diff_k_gain'], 512 // DIFF_DH).reshape(1, 512)

    tm = _tile(T, 512)
    t = _tile(S, 512)
    assert t >= 128
    gq, gk, gv, gr, la, dq, dk, dvt = _proj(
        x2, p['norm1_gain'].reshape(1, D), wm, wlow, wg2, bg2, bd, qg, kg, tm, t)

    y_gla = _gla(gq, gk, gv, gr, la, p['gla_out_gain'].reshape(1, GLA_DV), B, S, _tile(S, 256))

    rel_bias2 = rel_bias.astype(F32) * LOG2E
    y_diff = _diff(rel_bias2[REL_BUCKETS - 1],
                   dq.reshape(B, S, 512), dk.reshape(B, S, 512), dvt,
                   _bias_tiles(rel_bias2, t),
                   p['lam_q1'].reshape(1, -1), p['lam_k1'].reshape(1, -1),
                   p['lam_q2'].reshape(1, -1), p['lam_k2'].reshape(1, -1),
                   p['diff_sub_gain'].reshape(-1, 1), B, S, t, lam_init).reshape(T, 512)

    h1, m, e_t, gate_t, rank_t, counts = _mix(
        x2, y_gla, y_diff, p['w_out'].astype(BF16), p['norm2_gain'].reshape(1, D),
        p['w_router'].T, p['b_router'].reshape(-1, 1), tm)

    counts = counts.reshape(-1).astype(I32)
    padded = (counts + MOE_BLOCK - 1) // MOE_BLOCK * MOE_BLOCK
    pad_end = jnp.cumsum(padded)
    pad_start = pad_end - padded
    n_blocks = -(-(T * TOP_K) // MOE_BLOCK) + N_EXPERTS
    P = n_blocks * MOE_BLOCK
    blk_start = jnp.arange(n_blocks, dtype=I32) * MOE_BLOCK
    block_exp = jnp.minimum(jnp.sum((pad_end[None, :] <= blk_start[:, None]).astype(I32), axis=1),
                            N_EXPERTS - 1)
    n_used = (pad_end[-1:] // MOE_BLOCK).astype(I32)
    eids = jnp.arange(N_EXPERTS, dtype=I32)
    dest = rank_t + jnp.sum(jnp.where(e_t[..., None] == eids, pad_start, 0), axis=-1)

    tmd = _tile(T, 256)
    dest_tiles = dest.reshape(TOP_K, T // tmd, tmd).transpose(1, 0, 2).reshape(T // tmd, TOP_K * tmd)

    xs = _dispatch(pad_end, padded, dest_tiles, m, P, tmd)
    ys = _experts(block_exp, n_used, xs, p['w_gate_up'].astype(BF16),
                  p['b_gate_up'].reshape(N_EXPERTS, 1, -1), p['w_down'].astype(BF16),
                  p['b_down'].reshape(N_EXPERTS, 1, -1))
    out = _combine(dest_tiles, h1, gate_t.T, ys, tmd)
    return out.reshape(B, S, D)


def kernel(x, norm1_gain, w_in, w_gla_gate2, b_gla_gate2, gla_out_gain, diff_q_gain, diff_k_gain,
           lam_q1, lam_k1, lam_q2, lam_k2, diff_sub_gain, rel_bias, w_out, norm2_gain,
           w_router, b_router, w_gate_up, b_gate_up, w_down, b_down):
    stacked = dict(norm1_gain=norm1_gain, w_in=w_in, w_gla_gate2=w_gla_gate2,
                   b_gla_gate2=b_gla_gate2, gla_out_gain=gla_out_gain, diff_q_gain=diff_q_gain,
                   diff_k_gain=diff_k_gain, lam_q1=lam_q1, lam_k1=lam_k1, lam_q2=lam_q2,
                   lam_k2=lam_k2, diff_sub_gain=diff_sub_gain, w_out=w_out, norm2_gain=norm2_gain,
                   w_router=w_router, b_router=b_router, w_gate_up=w_gate_up,
                   b_gate_up=b_gate_up, w_down=w_down, b_down=b_down)
    h = x
    for l in range(norm1_gain.shape[0]):
        h = _layer(h, l, {k: v[l] for k, v in stacked.items()}, rel_bias)
    return h
```

```python
import functools
import math

import jax
import jax.numpy as jnp
from jax import lax
from jax.experimental import pallas as pl
from jax.experimental.pallas import tpu as pltpu

F32 = jnp.float32
BF16 = jnp.bfloat16
I32 = jnp.int32

EPS = 1e-6
GLA_HEADS = 4
GLA_DK = 64
GLA_DV = 128
GLA_RANK = 16
GLA_GATE_NORM = 16.0
GLA_CHUNK = 64
DIFF_HEADS = 4
DIFF_DH = 64
DIFF_DV = 128
REL_BUCKETS = 32
REL_MAX_DIST = 128
N_EXPERTS = 32
TOP_K = 4
SWIGLU_LIMIT = 7.0
SWIGLU_ALPHA = 1.702
MOE_BLOCK = 512
NEG_BIG = -1e30
LOG2E = math.log2(math.e)
ONES_ROWS = 16

VMEM_LIMIT = 56 * 1024 * 1024


def _cparams(sem, flags=None):
    return pltpu.CompilerParams(dimension_semantics=sem, vmem_limit_bytes=VMEM_LIMIT, flags=flags)


def _dot(a, b):
    return jnp.dot(a, b, preferred_element_type=F32)


def _dot_nt(a, b):
    return lax.dot_general(a, b, (((1,), (1,)), ((), ())), preferred_element_type=F32)


def _dot_tn(a, b):
    return lax.dot_general(a, b, (((0,), (0,)), ((), ())), preferred_element_type=F32)


def _split_bf16(x):
    hi = x.astype(BF16)
    lo = (x - hi.astype(F32)).astype(BF16)
    return hi, lo


SLAB = 8


def _store_slabs(ref, x):
    n = x.shape[0]
    for s in range(SLAB):
        ref[pl.ds(s, n, stride=SLAB), :] = x[:, s * 128:(s + 1) * 128]


def _load_slabs(ref, n):
    return jnp.concatenate([ref[pl.ds(s, n, stride=SLAB), :] for s in range(SLAB)], axis=1)


def _proj_kernel(x_ref, g1_ref, wm_ref, wlow_ref, wg2_ref, bg2_ref, bd_ref, qg_ref, kg_ref,
                 gq_ref, gk_ref, gv_ref, gr_ref, la_ref, dq_ref, dk_ref, dvt_ref, wf_ref, *, tm, tv):
    @pl.when(pl.program_id(0) == 0)
    def _():
        wf_ref[...] = _dot(wlow_ref[...].astype(BF16), wg2_ref[...].astype(BF16)).astype(BF16)

    x = x_ref[...]
    ms = jnp.mean(x * x, axis=-1, keepdims=True)
    n = (x * lax.rsqrt(ms + EPS) * g1_ref[...]).astype(BF16)

    def mm(a, b):
        return _dot(n, wm_ref[:, a:b])

    gq_ref[...] = (mm(0, 256) * (GLA_DK ** -0.5)).astype(BF16)
    gk_ref[...] = mm(256, 512).astype(BF16)
    gv_ref[...] = mm(512, 1024).astype(BF16)
    gr_ref[...] = mm(1024, 1536).astype(BF16)

    z = _dot(n, wf_ref[...]) + bg2_ref[...]
    la_ref[...] = (jnp.minimum(z, 0.0) - jnp.log(1.0 + jnp.exp(-jnp.abs(z)))) * (1.0 / GLA_GATE_NORM)

    def qknorm(y, gain):
        ss = _dot((y * y).astype(BF16), bd_ref[...])
        return y * lax.rsqrt(ss * (1.0 / DIFF_DH) + EPS) * gain

    dq_ref[...] = (qknorm(mm(1536, 2048), qg_ref[...]) * (DIFF_DH ** -0.5 * LOG2E)).astype(BF16)
    dk_ref[...] = qknorm(mm(2048, 2560), kg_ref[...]).astype(BF16)
    dv = mm(2560, 3072)
    for u in range(tm // tv):
        dvt_ref[u] = dv[u * tv:(u + 1) * tv, :].T.astype(BF16)


def _proj(x2, g1, wm, wlow, wg2, bg2, bd, qg, kg, tm, tv):
    T, D = x2.shape
    const = lambda i: (0, 0)
    row = lambda i: (i, 0)
    outs = [(256, BF16), (256, BF16), (512, BF16), (512, BF16), (256, F32),
            (512, BF16), (512, BF16)]
    return pl.pallas_call(
        functools.partial(_proj_kernel, tm=tm, tv=tv),
        grid=(T // tm,),
        in_specs=[
            pl.BlockSpec((tm, D), row),
            pl.BlockSpec((1, D), const),
            pl.BlockSpec(wm.shape, const),
            pl.BlockSpec(wlow.shape, const),
            pl.BlockSpec(wg2.shape, const),
            pl.BlockSpec(bg2.shape, const),
            pl.BlockSpec(bd.shape, const),
            pl.BlockSpec(qg.shape, const),
            pl.BlockSpec(kg.shape, const),
        ],
        out_specs=[pl.BlockSpec((tm, w), row) for w, _ in outs]
        + [pl.BlockSpec((tm // tv, 512, tv), lambda i: (i, 0, 0))],
        out_shape=[jax.ShapeDtypeStruct((T, w), dt) for w, dt in outs]
        + [jax.ShapeDtypeStruct((T // tv, 512, tv), BF16)],
        scratch_shapes=[pltpu.VMEM((D, 256), BF16)],
        compiler_params=_cparams(("arbitrary",)),
        name="proj",
    )(x2, g1, wm, wlow, wg2, bg2, bd, qg, kg)


def _gla_kernel(q_ref, k_ref, v_ref, r_ref, la_ref, gain_ref, o_ref, state_ref, *, tg):
    C = GLA_CHUNK

    @pl.when(pl.program_id(1) == 0)
    def _():
        state_ref[...] = jnp.zeros(state_ref.shape, F32)

    ri = lax.broadcasted_iota(I32, (C, C), 0)
    ci = lax.broadcasted_iota(I32, (C, C), 1)
    causal = ri >= ci
    tril = jnp.where(causal, 1.0, 0.0).astype(BF16)
    ones_cv = jnp.ones((C, GLA_DV), BF16)
    gain = gain_ref[...]

    for c in range(tg // C):
        sl = slice(c * C, (c + 1) * C)
        la_hi, la_lo = _split_bf16(la_ref[sl, :])
        b = _dot(tril, la_hi) + _dot(tril, la_lo)
        b_last = b[C - 1:C, :]
        q = q_ref[sl, :].astype(F32)
        k = k_ref[sl, :].astype(F32)
        q_dec = (q * jnp.exp(b)).astype(BF16)
        k_dec = (k * jnp.exp(-b)).astype(BF16)
        k_last = (k * jnp.exp(b_last - b)).astype(BF16)
        for h in range(GLA_HEADS):
            hs = slice(h * GLA_DK, (h + 1) * GLA_DK)
            vs = slice(h * GLA_DV, (h + 1) * GLA_DV)
            v = v_ref[sl, vs]
            att = jnp.where(causal, _dot_nt(q_dec[:, hs], k_dec[:, hs]), 0.0)
            state = state_ref[h]
            o = _dot(att.astype(BF16), v) + _dot(q_dec[:, hs], state.astype(BF16))
            dcol = _dot_tn(la_hi[:, hs], ones_cv) + _dot_tn(la_lo[:, hs], ones_cv)
            state_ref[h] = state * jnp.exp(dcol) + _dot_tn(k_last[:, hs], v)
            on = o * lax.rsqrt(jnp.mean(o * o, axis=-1, keepdims=True) + EPS) * gain
            r = r_ref[sl, vs].astype(F32)
            o_ref[sl, vs] = (on * (r * jax.nn.sigmoid(r))).astype(o_ref.dtype)


def _gla(gq, gk, gv, gr, la, gain, B, S, tg):
    nt = S // tg
    row = lambda b, t: (b * nt + t, 0)
    const = lambda b, t: (0, 0)
    return pl.pallas_call(
        functools.partial(_gla_kernel, tg=tg),
        grid=(B, nt),
        in_specs=[
            pl.BlockSpec((tg, 256), row),
            pl.BlockSpec((tg, 256), row),
            pl.BlockSpec((tg, 512), row),
            pl.BlockSpec((tg, 512), row),
            pl.BlockSpec((tg, 256), row),
            pl.BlockSpec((1, GLA_DV), const),
        ],
        out_specs=pl.BlockSpec((tg, 512), row),
        out_shape=jax.ShapeDtypeStruct((B * S, 512), BF16),
        scratch_shapes=[pltpu.VMEM((GLA_HEADS, GLA_DK, GLA_DV), F32)],
        compiler_params=_cparams(("arbitrary", "arbitrary")),
        name="gla",
    )(gq, gk, gv, gr, la, gain)


def _diff_kernel(far_ref, q_ref, k_ref, vt_ref, bias_ref, lq1_ref, lk1_ref, lq2_ref, lk2_ref,
                 sg_ref, o_ref, qs_ref, m_ref, acc_ref, s_ref, smax_ref, *, t, qc, lam_init):
    h = pl.program_id(1)
    i = pl.program_id(2)
    q = q_ref[...]
    lane = lax.broadcasted_iota(I32, q.shape, 1)
    zero = jnp.zeros_like(q)
    qs_ref[0:t, :] = jnp.where(lane < DIFF_DH, q, zero)
    qs_ref[t:2 * t, :] = jnp.where(lane >= DIFF_DH, q, zero)
    m_ref[...] = jnp.full(m_ref.shape, NEG_BIG, F32)
    acc_ref[...] = jnp.zeros(acc_ref.shape, F32)
    ones_rows = jnp.ones((ONES_ROWS, t), BF16)

    nc = 2 * t // qc
    far_bias = far_ref[h]

    def chunk(c):
        return slice(c * qc, (c + 1) * qc)

    def keys(j):
        return k_ref[pl.ds(pl.multiple_of(j * t, t), t), :]

    def values(j):
        return jnp.concatenate([vt_ref[j], ones_rows], axis=0)

    def scores(kj, c):
        return _dot_nt(kj, qs_ref[chunk(c), :])

    def update(c, s, smax, shift, vtj):
        m_prev = m_ref[:, chunk(c)]
        m_new = jnp.maximum(m_prev, smax + shift)
        alpha = jnp.exp2(m_prev - m_new)
        p = jnp.exp2((s - (m_new - shift)).astype(BF16))
        acc_ref[:, chunk(c)] = alpha * acc_ref[:, chunk(c)] + _dot(vtj, p)
        m_ref[:, chunk(c)] = m_new

    def near_steps(blocks):
        items = []
        for slot, (j, tile_idx) in enumerate(blocks):
            kj, vtj = keys(j), values(j)
            for c in range(nc):
                c0 = (c * qc) % t
                n_keys = c0 + qc if tile_idx == 0 else t
                items.append((c, c0, n_keys, tile_idx, slot, kj, vtj))

        def stage(item):
            c, c0, n_keys, tile_idx, slot, kj, _ = item
            s = scores(kj[:n_keys], c) + bias_ref[tile_idx, :n_keys, c0:c0 + qc]
            s_ref[slot, :n_keys, chunk(c)] = s
            smax_ref[slot, :, chunk(c)] = jnp.max(s, axis=0, keepdims=True)

        stage(items[0])
        for n, (c, _, n_keys, _, slot, _, vtj) in enumerate(items):
            if n + 1 < len(items):
                stage(items[n + 1])
            update(c, s_ref[slot, :n_keys, chunk(c)], smax_ref[slot, :, chunk(c)], 0.0,
                   vtj[:, :n_keys])

    def far_scores(kj, c, slot):
        s = scores(kj, c)
        s_ref[slot, :, chunk(c)] = s
        smax_ref[slot, :, chunk(c)] = jnp.max(s, axis=0, keepdims=True)

    def far_update(c, slot, vtj):
        update(c, s_ref[slot, :, chunk(c)], smax_ref[slot, :, chunk(c)], far_bias, vtj)

    n_far = i - 1

    def far_phase(j, slot, prefetch):
        vtj = values(j)
        if prefetch:
            k_next = keys(j + 1)
        for c in range(nc):
            if prefetch:
                far_scores(k_next, c, 1 - slot)
            far_update(c, slot, vtj)

    @pl.when(n_far > 0)
    def _():
        k0 = keys(0)
        for c in range(nc):
            far_scores(k0, c, 0)

        n_pairs = (n_far - 1) // 2

        def body(pair, carry):
            far_phase(2 * pair, 0, True)
            far_phase(2 * pair + 1, 1, True)
            return carry

        lax.fori_loop(0, n_pairs, body, 0)
        done = 2 * n_pairs

        @pl.when(n_far - done == 1)
        def _():
            far_phase(done, 0, False)

        @pl.when(n_far - done == 2)
        def _():
            far_phase(done, 0, True)
            far_phase(done + 1, 1, False)

    @pl.when(i > 0)
    def _():
        near_steps([(i - 1, 1), (i, 0)])

    @pl.when(i == 0)
    def _():
        near_steps([(i, 0)])

    lam = (jnp.exp(jnp.sum(lq1_ref[...] * lk1_ref[...], axis=-1, keepdims=True))
           - jnp.exp(jnp.sum(lq2_ref[...] * lk2_ref[...], axis=-1, keepdims=True)) + lam_init)
    out = acc_ref[0:DIFF_DV, :] / acc_ref[DIFF_DV:DIFF_DV + 1, :]
    o = out[:, :t] - lam * out[:, t:]
    on = o * lax.rsqrt(jnp.mean(o * o, axis=0, keepdims=True) + EPS) * sg_ref[...]
    o_ref[...] = (on * (1.0 - lam_init)).T.astype(o_ref.dtype)


def _diff(far_bias, dq, dk, dvt, bias_tiles, lq1, lk1, lq2, lk2, sg, B, S, t, lam_init):
    nq = S // t
    vec = lambda b, h, i: (0, 0)
    return pl.pallas_call(
        functools.partial(_diff_kernel, t=t, qc=min(t, 256), lam_init=lam_init),
        grid=(B, DIFF_HEADS, nq),
        in_specs=[
            pl.BlockSpec(memory_space=pltpu.SMEM),
            pl.BlockSpec((None, t, 128), lambda b, h, i: (b, i, h)),
            pl.BlockSpec((None, S, 128), lambda b, h, i: (b, 0, h)),
            pl.BlockSpec((nq, DIFF_DV, t), lambda b, h, i: (b, h, 0)),
            pl.BlockSpec((None, 2, t, t), lambda b, h, i: (h, 0, 0, 0)),
            pl.BlockSpec((1, DIFF_DH), vec),
            pl.BlockSpec((1, DIFF_DH), vec),
            pl.BlockSpec((1, DIFF_DH), vec),
            pl.BlockSpec((1, DIFF_DH), vec),
            pl.BlockSpec((DIFF_DV, 1), vec),
        ],
        out_specs=pl.BlockSpec((None, t, 128), lambda b, h, i: (b, i, h)),
        out_shape=jax.ShapeDtypeStruct((B, S, 512), BF16),
        scratch_shapes=[pltpu.VMEM((2 * t, 128), BF16), pltpu.VMEM((1, 2 * t), F32),
                        pltpu.VMEM((DIFF_DV + ONES_ROWS, 2 * t), F32),
                        pltpu.VMEM((2, t, 2 * t), F32), pltpu.VMEM((2, 1, 2 * t), F32)],
        compiler_params=_cparams(("arbitrary", "arbitrary", "arbitrary")),
        name="diff_attn",
    )(far_bias, dq, dk, dvt, bias_tiles, lq1, lk1, lq2, lk2, sg)


def _t5_causal_bucket(n):
    max_exact = REL_BUCKETS // 2
    nf = jnp.maximum(n, 1).astype(F32)
    large = max_exact + (jnp.log(nf / max_exact) / math.log(REL_MAX_DIST / max_exact)
                         * (REL_BUCKETS - max_exact)).astype(I32)
    large = jnp.minimum(large, REL_BUCKETS - 1)
    return jnp.where(n < max_exact, n, large)


def _bias_tiles(rel_bias, t):
    kj = jnp.arange(t)[:, None]
    qi = jnp.arange(t)[None, :]
    d0 = qi - kj

    def lookup(bucket):
        hot = bucket[None, :, :, None] == jnp.arange(REL_BUCKETS)
        return jnp.sum(jnp.where(hot, rel_bias.T[:, None, None, :], 0.0), axis=-1)

    diag = jnp.where(d0 >= 0, lookup(_t5_causal_bucket(jnp.maximum(d0, 0))), NEG_BIG)
    sub = lookup(_t5_causal_bucket(d0 + t))
    return jnp.stack([diag, sub], axis=1).astype(F32)


def _mix_kernel(x_ref, yg_ref, yd_ref, wo_ref, g2_ref, wr_ref, br_ref,
                h_ref, m_ref, e_ref, gate_ref, rank_ref, cnt_ref, base_ref, *, tm):
    @pl.when(pl.program_id(0) == 0)
    def _():
        base_ref[...] = jnp.zeros(base_ref.shape, F32)

    h1 = x_ref[...] + _dot(yg_ref[...], wo_ref[0:512, :]) + _dot(yd_ref[...], wo_ref[512:1024, :])
    h_ref[...] = h1
    ms = jnp.mean(h1 * h1, axis=-1, keepdims=True)
    m = h1 * lax.rsqrt(ms + EPS) * g2_ref[...]
    _store_slabs(m_ref, m)

    m_hi, m_lo = _split_bf16(m)
    w_hi, w_lo = _split_bf16(wr_ref[...])
    logits = _dot_nt(w_hi, m_hi) + _dot_nt(w_hi, m_lo) + _dot_nt(w_lo, m_hi) + br_ref[...]

    eidx = lax.broadcasted_iota(I32, logits.shape, 0)
    work = logits
    vals, idxs = [], []
    for _ in range(TOP_K):
        mx = jnp.max(work, axis=0, keepdims=True)
        ix = jnp.min(jnp.where(work == mx, eidx, N_EXPERTS), axis=0, keepdims=True)
        vals.append(mx)
        idxs.append(ix)
        work = jnp.where(eidx == ix, -jnp.inf, work)
    ex = [jnp.exp(v - vals[0]) for v in vals]
    den = ex[0] + ex[1] + ex[2] + ex[3]
    gate_ref[...] = jnp.concatenate([e / den for e in ex], axis=0)
    e_ref[...] = jnp.concatenate(idxs, axis=0)

    onehots = [jnp.where(eidx == ix, 1.0, 0.0) for ix in idxs]
    cnt = onehots[0] + onehots[1] + onehots[2] + onehots[3]
    ti = lax.broadcasted_iota(I32, (tm, tm), 0)
    tj = lax.broadcasted_iota(I32, (tm, tm), 1)
    upper = jnp.where(ti < tj, 1.0, 0.0).astype(BF16)
    prefix = _dot(cnt.astype(BF16), upper) + base_ref[...]
    rank_ref[...] = jnp.concatenate(
        [jnp.sum(oh * prefix, axis=0, keepdims=True) for oh in onehots], axis=0).astype(I32)
    base_ref[...] = base_ref[...] + jnp.sum(cnt, axis=1, keepdims=True)
    cnt_ref[...] = base_ref[...]


def _mix(x2, yg, yd, wo, g2, wr_t, br, tm):
    T, D = x2.shape
    row = lambda i: (i, 0)
    col = lambda i: (0, i)
    const = lambda i: (0, 0)
    return pl.pallas_call(
        functools.partial(_mix_kernel, tm=tm),
        grid=(T // tm,),
        in_specs=[
            pl.BlockSpec((tm, D), row),
            pl.BlockSpec((tm, 512), row),
            pl.BlockSpec((tm, 512), row),
            pl.BlockSpec(wo.shape, const),
            pl.BlockSpec((1, D), const),
            pl.BlockSpec(wr_t.shape, const),
            pl.BlockSpec(br.shape, const),
        ],
        out_specs=[
            pl.BlockSpec((tm, D), row),
            pl.BlockSpec((tm * SLAB, 128), row),
            pl.BlockSpec((TOP_K, tm), col),
            pl.BlockSpec((TOP_K, tm), col),
            pl.BlockSpec((TOP_K, tm), col),
            pl.BlockSpec((N_EXPERTS, 1), const),
        ],
        out_shape=[
            jax.ShapeDtypeStruct((T, D), F32),
            jax.ShapeDtypeStruct((T * SLAB, 128), F32),
            jax.ShapeDtypeStruct((TOP_K, T), I32),
            jax.ShapeDtypeStruct((TOP_K, T), F32),
            jax.ShapeDtypeStruct((TOP_K, T), I32),
            jax.ShapeDtypeStruct((N_EXPERTS, 1), F32),
        ],
        scratch_shapes=[pltpu.VMEM((N_EXPERTS, 1), F32)],
        compiler_params=_cparams(("arbitrary",)),
        name="mix_router",
    )(x2, yg, yd, wo, g2, wr_t, br)


def _slab(ref, row):
    return ref.at[pl.ds(pl.multiple_of(row * SLAB, SLAB), SLAB)]


def _slab_wait_all(src_ref, dst_ref, sem, n):
    def body(r, c):
        pltpu.make_async_copy(_slab(src_ref, 0), _slab(dst_ref, 0), sem).wait()
        return c
    lax.fori_loop(0, n, body, 0, unroll=8)


def _dispatch_kernel(pend_ref, padded_ref, dest_hbm, m_ref, xs_hbm, dest_smem, zero_ref,
                     sem_idx, sem_zero, sem, *, tm):
    i = pl.program_id(0)

    @pl.when(i == 0)
    def _():
        zero_ref[...] = jnp.zeros(zero_ref.shape, F32)

        def last_block(e):
            return xs_hbm.at[pl.ds(pl.multiple_of((pend_ref[e] - MOE_BLOCK) * SLAB, SLAB),
                                   MOE_BLOCK * SLAB)]

        def start(e, c):
            @pl.when(padded_ref[e] > 0)
            def _():
                pltpu.make_async_copy(zero_ref, last_block(e), sem_zero).start()
            return c

        def wait(e, c):
            @pl.when(padded_ref[e] > 0)
            def _():
                pltpu.make_async_copy(zero_ref, last_block(e), sem_zero).wait()
            return c

        lax.fori_loop(0, N_EXPERTS, start, 0)
        lax.fori_loop(0, N_EXPERTS, wait, 0)

    cp = pltpu.make_async_copy(dest_hbm.at[i], dest_smem, sem_idx)
    cp.start()
    cp.wait()

    def body(r, c):
        for k in range(TOP_K):
            d = dest_smem[k * tm + r]
            pltpu.make_async_copy(_slab(m_ref, r), _slab(xs_hbm, d), sem).start()
        return c
    lax.fori_loop(0, tm, body, 0, unroll=4)
    _slab_wait_all(m_ref, xs_hbm, sem, TOP_K * tm)


def _dispatch(pad_end, padded, dest_tiles, m_slabs, P, tm):
    T = m_slabs.shape[0] // SLAB
    return pl.pallas_call(
        functools.partial(_dispatch_kernel, tm=tm),
        grid_spec=pltpu.PrefetchScalarGridSpec(
            num_scalar_prefetch=2,
            grid=(T // tm,),
            in_specs=[
                pl.BlockSpec(memory_space=pl.ANY),
                pl.BlockSpec((tm * SLAB, 128), lambda i, pe, pd: (i, 0)),
            ],
            out_specs=pl.BlockSpec(memory_space=pl.ANY),
            scratch_shapes=[pltpu.SMEM((TOP_K * tm,), I32),
                            pltpu.VMEM((MOE_BLOCK * SLAB, 128), F32),
                            pltpu.SemaphoreType.DMA, pltpu.SemaphoreType.DMA,
                            pltpu.SemaphoreType.DMA],
        ),
        out_shape=jax.ShapeDtypeStruct((P * SLAB, 128), F32),
        compiler_params=_cparams(("arbitrary",)),
        name="moe_dispatch",
    )(pad_end, padded, dest_tiles, m_slabs)


def _experts_kernel(bexp_ref, nused_ref, xs_ref, wgu_ref, bgu_ref, wd_ref, bd_ref, ys_ref, *, dff):
    del bexp_ref

    @pl.when(pl.program_id(0) < nused_ref[0])
    def _():
        x = _load_slabs(xs_ref, MOE_BLOCK).astype(BF16)
        gu = _dot(x, wgu_ref[...]) + bgu_ref[...]
        glu = jnp.minimum(gu[:, :dff], SWIGLU_LIMIT)
        lin = jnp.clip(gu[:, dff:], -SWIGLU_LIMIT, SWIGLU_LIMIT)
        hid = glu * jax.nn.sigmoid(SWIGLU_ALPHA * glu) * (lin + 1.0)
        _store_slabs(ys_ref, _dot(hid.astype(BF16), wd_ref[...]) + bd_ref[...])


def _experts(block_exp, n_used, xs, wgu, bgu, wd, bd):
    D, dff = wd.shape[2], wd.shape[1]
    assert D == SLAB * 128
    P = xs.shape[0] // SLAB
    nb = P // MOE_BLOCK
    blk = lambda i, be, nu: (jnp.minimum(i, nu[0] - 1), 0)
    exp3 = lambda i, be, nu: (be[i], 0, 0)
    return pl.pallas_call(
        functools.partial(_experts_kernel, dff=dff),
        grid_spec=pltpu.PrefetchScalarGridSpec(
            num_scalar_prefetch=2,
            grid=(nb,),
            in_specs=[
                pl.BlockSpec((MOE_BLOCK * SLAB, 128), blk),
                pl.BlockSpec((None, D, 2 * dff), exp3),
                pl.BlockSpec((None, 1, 2 * dff), exp3),
                pl.BlockSpec((None, dff, D), exp3),
                pl.BlockSpec((None, 1, D), exp3),
            ],
            out_specs=pl.BlockSpec((MOE_BLOCK * SLAB, 128), blk),
        ),
        out_shape=jax.ShapeDtypeStruct((P * SLAB, 128), F32),
        compiler_params=_cparams(("arbitrary",)),
        name="moe_experts",
    )(block_exp, n_used, xs, wgu, bgu, wd, bd)


def _combine_kernel(dest_hbm, h_ref, gate_ref, ys_hbm, o_ref, buf, dest_smem, sem_idx, sem,
                    *, tm, nt):
    i = pl.program_id(0)
    slot = lax.rem(i, 2)

    def fetch(tile, slot):
        cp = pltpu.make_async_copy(dest_hbm.at[tile], dest_smem, sem_idx)
        cp.start()
        cp.wait()

        def body(r, c):
            for k in range(TOP_K):
                d = dest_smem[k * tm + r]
                pltpu.make_async_copy(_slab(ys_hbm, d), _slab(buf.at[slot, k], r),
                                      sem.at[slot]).start()
            return c
        lax.fori_loop(0, tm, body, 0, unroll=4)

    @pl.when(i == 0)
    def _():
        fetch(0, 0)

    @pl.when(i + 1 < nt)
    def _():
        fetch(i + 1, 1 - slot)

    _slab_wait_all(ys_hbm, buf.at[slot, 0], sem.at[slot], TOP_K * tm)

    g = gate_ref[...]
    acc = h_ref[...]
    for k in range(TOP_K):
        acc = acc + g[:, k:k + 1] * _load_slabs(buf.at[slot, k], tm)
    o_ref[...] = acc


def _combine(dest_tiles, h1, gates_t, ys, tm):
    T, D = h1.shape
    nt = T // tm
    return pl.pallas_call(
        functools.partial(_combine_kernel, tm=tm, nt=nt),
        grid=(nt,),
        in_specs=[
            pl.BlockSpec(memory_space=pl.ANY),
            pl.BlockSpec((tm, D), lambda i: (i, 0)),
            pl.BlockSpec((tm, TOP_K), lambda i: (i, 0)),
            pl.BlockSpec(memory_space=pl.ANY),
        ],
        out_specs=pl.BlockSpec((tm, D), lambda i: (i, 0)),
        out_shape=jax.ShapeDtypeStruct((T, D), F32),
        scratch_shapes=[pltpu.VMEM((2, TOP_K, tm * SLAB, 128), F32),
                        pltpu.SMEM((TOP_K * tm,), I32),
                        pltpu.SemaphoreType.DMA, pltpu.SemaphoreType.DMA((2,))],
        compiler_params=_cparams(("arbitrary",)),
        name="moe_combine",
    )(dest_tiles, h1, gates_t, ys)


def _tile(n, pref):
    t = min(n, pref)
    assert n % t == 0, (n, pref)
    return t


def _layer(h, l, p, rel_bias):
    B, S, D = h.shape
    T = B * S
    x2 = h.reshape(T, D)
    lam_init = 0.8 - 0.6 * math.exp(-0.3 * l)

    w_in = p['w_in']
    c_low = 1024
    wm = jnp.concatenate([w_in[:, :c_low], w_in[:, c_low + GLA_RANK:]], axis=1).astype(BF16)
    wlow = jnp.pad(w_in[:, c_low:c_low + GLA_RANK], ((0, 0), (0, 128 - GLA_RANK)))
    wg2 = jnp.pad(p['w_gla_gate2'], ((0, 128 - GLA_RANK), (0, 0)))
    bg2 = p['b_gla_gate2'].reshape(1, -1)
    gidx = jnp.arange(512) // DIFF_DH
    bd = (gidx[:, None] == gidx[None, :]).astype(BF16)
    qg = jnp.tile(p['diff_q_gain'], 512 // DIFF_DH).reshape(1, 512)
    kg = jnp.tile(p['diff_k_gain'], 512 // DIFF_DH).reshape(1, 512)

    tm = _tile(T, 512)
    t = _tile(S, 512)
    assert t >= 128
    gq, gk, gv, gr, la, dq, dk, dvt = _proj(
        x2, p['norm1_gain'].reshape(1, D), wm, wlow, wg2, bg2, bd, qg, kg, tm, t)

    y_gla = _gla(gq, gk, gv, gr, la, p['gla_out_gain'].reshape(1, GLA_DV), B, S, _tile(S, 256))

    rel_bias2 = rel_bias.astype(F32) * LOG2E
    y_diff = _diff(rel_bias2[REL_BUCKETS - 1],
                   dq.reshape(B, S, 512), dk.reshape(B, S, 512), dvt,
                   _bias_tiles(rel_bias2, t),
                   p['lam_q1'].reshape(1, -1), p['lam_k1'].reshape(1, -1),
                   p['lam_q2'].reshape(1, -1), p['lam_k2'].reshape(1, -1),
                   p['diff_sub_gain'].reshape(-1, 1), B, S, t, lam_init).reshape(T, 512)

    h1, m, e_t, gate_t, rank_t, counts = _mix(
        x2, y_gla, y_diff, p['w_out'].astype(BF16), p['norm2_gain'].reshape(1, D),
        p['w_router'].T, p['b_router'].reshape(-1, 1), tm)

    counts = counts.reshape(-1).astype(I32)
    padded = (counts + MOE_BLOCK - 1) // MOE_BLOCK * MOE_BLOCK
    pad_end = jnp.cumsum(padded)
    pad_start = pad_end - padded
    n_blocks = -(-(T * TOP_K) // MOE_BLOCK) + N_EXPERTS
    P = n_blocks * MOE_BLOCK
    blk_start = jnp.arange(n_blocks, dtype=I32) * MOE_BLOCK
    block_exp = jnp.minimum(jnp.sum((pad_end[None, :] <= blk_start[:, None]).astype(I32), axis=1),
                            N_EXPERTS - 1)
    n_used = (pad_end[-1:] // MOE_BLOCK).astype(I32)
    eids = jnp.arange(N_EXPERTS, dtype=I32)
    dest = rank_t + jnp.sum(jnp.where(e_t[..., None] == eids, pad_start, 0), axis=-1)

    tmd = _tile(T, 256)
    dest_tiles = dest.reshape(TOP_K, T // tmd, tmd).transpose(1, 0, 2).reshape(T // tmd, TOP_K * tmd)

    xs = _dispatch(pad_end, padded, dest_tiles, m, P, tmd)
    ys = _experts(block_exp, n_used, xs, p['w_gate_up'].astype(BF16),
                  p['b_gate_up'].reshape(N_EXPERTS, 1, -1), p['w_down'].astype(BF16),
                  p['b_down'].reshape(N_EXPERTS, 1, -1))
    out = _combine(dest_tiles, h1, gate_t.T, ys, tmd)
    return out.reshape(B, S, D)


def kernel(x, norm1_gain, w_in, w_gla_gate2, b_gla_gate2, gla_out_gain, diff_q_gain, diff_k_gain,
           lam_q1, lam_k1, lam_q2, lam_k2, diff_sub_gain, rel_bias, w_out, norm2_gain,
           w_router, b_router, w_gate_up, b_gate_up, w_down, b_down):
    stacked = dict(norm1_gain=norm1_gain, w_in=w_in, w_gla_gate2=w_gla_gate2,
                   b_gla_gate2=b_gla_gate2, gla_out_gain=gla_out_gain, diff_q_gain=diff_q_gain,
                   diff_k_gain=diff_k_gain, lam_q1=lam_q1, lam_k1=lam_k1, lam_q2=lam_q2,
                   lam_k2=lam_k2, diff_sub_gain=diff_sub_gain, w_out=w_out, norm2_gain=norm2_gain,
                   w_router=w_router, b_router=b_router, w_gate_up=w_gate_up,
                   b_gate_up=b_gate_up, w_down=w_down, b_down=b_down)
    h = x
    for l in range(norm1_gain.shape[0]):
        h = _layer(h, l, {k: v[l] for k, v in stacked.items()}, rel_bias)
    return h
```

```python
import functools
import math

import jax
import jax.numpy as jnp
from jax import lax
from jax.experimental import pallas as pl
from jax.experimental.pallas import tpu as pltpu

F32 = jnp.float32
BF16 = jnp.bfloat16
I32 = jnp.int32

EPS = 1e-6
GLA_HEADS = 4
GLA_DK = 64
GLA_DV = 128
GLA_RANK = 16
GLA_GATE_NORM = 16.0
GLA_CHUNK = 64
DIFF_HEADS = 4
DIFF_DH = 64
DIFF_DV = 128
REL_BUCKETS = 32
REL_MAX_DIST = 128
N_EXPERTS = 32
TOP_K = 4
SWIGLU_LIMIT = 7.0
SWIGLU_ALPHA = 1.702
MOE_BLOCK = 512
NEG_BIG = -1e30
LOG2E = math.log2(math.e)
ONES_ROWS = 16

VMEM_LIMIT = 56 * 1024 * 1024


def _cparams(sem, flags=None):
    return pltpu.CompilerParams(dimension_semantics=sem, vmem_limit_bytes=VMEM_LIMIT, flags=flags)


def _dot(a, b):
    return jnp.dot(a, b, preferred_element_type=F32)


def _dot_nt(a, b):
    return lax.dot_general(a, b, (((1,), (1,)), ((), ())), preferred_element_type=F32)


def _dot_tn(a, b):
    return lax.dot_general(a, b, (((0,), (0,)), ((), ())), preferred_element_type=F32)


def _split_bf16(x):
    hi = x.astype(BF16)
    lo = (x - hi.astype(F32)).astype(BF16)
    return hi, lo


SLAB = 8


def _store_slabs(ref, x):
    n = x.shape[0]
    for s in range(SLAB):
        ref[pl.ds(s, n, stride=SLAB), :] = x[:, s * 128:(s + 1) * 128]


def _load_slabs(ref, n):
    return jnp.concatenate([ref[pl.ds(s, n, stride=SLAB), :] for s in range(SLAB)], axis=1)


def _proj_kernel(x_ref, g1_ref, wm_ref, wlow_ref, wg2_ref, bg2_ref, bd_ref, qg_ref, kg_ref,
                 gq_ref, gk_ref, gv_ref, gr_ref, la_ref, dq_ref, dk_ref, dvt_ref, wf_ref, *, tm, tv):
    @pl.when(pl.program_id(0) == 0)
    def _():
        wf_ref[...] = _dot(wlow_ref[...].astype(BF16), wg2_ref[...].astype(BF16)).astype(BF16)

    x = x_ref[...]
    ms = jnp.mean(x * x, axis=-1, keepdims=True)
    n = (x * lax.rsqrt(ms + EPS) * g1_ref[...]).astype(BF16)

    def mm(a, b):
        return _dot(n, wm_ref[:, a:b])

    gq_ref[...] = (mm(0, 256) * (GLA_DK ** -0.5)).astype(BF16)
    gk_ref[...] = mm(256, 512).astype(BF16)
    gv_ref[...] = mm(512, 1024).astype(BF16)
    gr_ref[...] = mm(1024, 1536).astype(BF16)

    z = _dot(n, wf_ref[...]) + bg2_ref[...]
    la_ref[...] = (jnp.minimum(z, 0.0) - jnp.log(1.0 + jnp.exp(-jnp.abs(z)))) * (1.0 / GLA_GATE_NORM)

    def qknorm(y, gain):
        ss = _dot((y * y).astype(BF16), bd_ref[...])
        return y * lax.rsqrt(ss * (1.0 / DIFF_DH) + EPS) * gain

    dq_ref[...] = (qknorm(mm(1536, 2048), qg_ref[...]) * (DIFF_DH ** -0.5 * LOG2E)).astype(BF16)
    dk_ref[...] = qknorm(mm(2048, 2560), kg_ref[...]).astype(BF16)
    dv = mm(2560, 3072)
    for u in range(tm // tv):
        dvt_ref[u] = dv[u * tv:(u + 1) * tv, :].T.astype(BF16)


def _proj(x2, g1, wm, wlow, wg2, bg2, bd, qg, kg, tm, tv):
    T, D = x2.shape
    const = lambda i: (0, 0)
    row = lambda i: (i, 0)
    outs = [(256, BF16), (256, BF16), (512, BF16), (512, BF16), (256, F32),
            (512, BF16), (512, BF16)]
    return pl.pallas_call(
        functools.partial(_proj_kernel, tm=tm, tv=tv),
        grid=(T // tm,),
        in_specs=[
            pl.BlockSpec((tm, D), row),
            pl.BlockSpec((1, D), const),
            pl.BlockSpec(wm.shape, const),
            pl.BlockSpec(wlow.shape, const),
            pl.BlockSpec(wg2.shape, const),
            pl.BlockSpec(bg2.shape, const),
            pl.BlockSpec(bd.shape, const),
            pl.BlockSpec(qg.shape, const),
            pl.BlockSpec(kg.shape, const),
        ],
        out_specs=[pl.BlockSpec((tm, w), row) for w, _ in outs]
        + [pl.BlockSpec((tm // tv, 512, tv), lambda i: (i, 0, 0))],
        out_shape=[jax.ShapeDtypeStruct((T, w), dt) for w, dt in outs]
        + [jax.ShapeDtypeStruct((T // tv, 512, tv), BF16)],
        scratch_shapes=[pltpu.VMEM((D, 256), BF16)],
        compiler_params=_cparams(("arbitrary",)),
        name="proj",
    )(x2, g1, wm, wlow, wg2, bg2, bd, qg, kg)


def _gla_kernel(q_ref, k_ref, v_ref, r_ref, la_ref, gain_ref, o_ref, state_ref, *, tg):
    C = GLA_CHUNK

    @pl.when(pl.program_id(1) == 0)
    def _():
        state_ref[...] = jnp.zeros(state_ref.shape, F32)

    nchunk = tg // C
    chunks = [slice(c * C, (c + 1) * C) for c in range(nchunk)]
    heads = [(slice(h * GLA_DK, (h + 1) * GLA_DK), slice(h * GLA_DV, (h + 1) * GLA_DV))
             for h in range(GLA_HEADS)]
    gain = gain_ref[...]

    ri = lax.broadcasted_iota(I32, (tg, tg), 0)
    ci = lax.broadcasted_iota(I32, (tg, tg), 1)
    shift = C.bit_length() - 1
    same_chunk = lax.shift_right_logical(ri, shift) == lax.shift_right_logical(ci, shift)
    tril = jnp.where(same_chunk & (ri >= ci), 1.0, 0.0).astype(BF16)
    causal = lax.broadcasted_iota(I32, (C, C), 0) >= lax.broadcasted_iota(I32, (C, C), 1)
    ones_cv = jnp.ones((C, GLA_DV), BF16)

    la_hi, la_lo = _split_bf16(la_ref[...])
    b = _dot(tril, la_hi) + _dot(tril, la_lo)
    ones_bd = jnp.where(same_chunk, 1.0, 0.0).astype(BF16)
    b_last = _dot(ones_bd, la_hi) + _dot(ones_bd, la_lo)
    q = q_ref[...].astype(F32)
    k = k_ref[...].astype(F32)
    q_dec = (q * jnp.exp(b)).astype(BF16)
    k_dec = (k * jnp.exp(-b)).astype(BF16)
    k_last = (k * jnp.exp(b_last - b)).astype(BF16)

    att = {(c, h): jnp.where(causal, _dot_nt(q_dec[sl, hs], k_dec[sl, hs]), 0.0).astype(BF16)
           for c, sl in enumerate(chunks) for h, (hs, _) in enumerate(heads)}
    o_intra = {(c, h): _dot(att[c, h], v_ref[sl, vs])
               for c, sl in enumerate(chunks) for h, (_, vs) in enumerate(heads)}
    kv = {(c, h): _dot_tn(k_last[sl, hs], v_ref[sl, vs])
          for c, sl in enumerate(chunks) for h, (hs, vs) in enumerate(heads)}
    decay = [jnp.exp(_dot_tn(la_hi[sl, :], ones_cv) + _dot_tn(la_lo[sl, :], ones_cv))
             for sl in chunks]

    states = [state_ref[h] for h in range(GLA_HEADS)]
    for c, sl in enumerate(chunks):
        for h, (hs, vs) in enumerate(heads):
            o = o_intra[c, h] + _dot(q_dec[sl, hs], states[h].astype(BF16))
            states[h] = states[h] * decay[c][hs, :] + kv[c, h]
            on = o * lax.rsqrt(jnp.mean(o * o, axis=-1, keepdims=True) + EPS) * gain
            r = r_ref[sl, vs].astype(F32)
            o_ref[sl, vs] = (on * (r * jax.nn.sigmoid(r))).astype(o_ref.dtype)
    for h in range(GLA_HEADS):
        state_ref[h] = states[h]


def _gla(gq, gk, gv, gr, la, gain, B, S, tg):
    nt = S // tg
    row = lambda b, t: (b * nt + t, 0)
    const = lambda b, t: (0, 0)
    return pl.pallas_call(
        functools.partial(_gla_kernel, tg=tg),
        grid=(B, nt),
        in_specs=[
            pl.BlockSpec((tg, 256), row),
            pl.BlockSpec((tg, 256), row),
            pl.BlockSpec((tg, 512), row),
            pl.BlockSpec((tg, 512), row),
            pl.BlockSpec((tg, 256), row),
            pl.BlockSpec((1, GLA_DV), const),
        ],
        out_specs=pl.BlockSpec((tg, 512), row),
        out_shape=jax.ShapeDtypeStruct((B * S, 512), BF16),
        scratch_shapes=[pltpu.VMEM((GLA_HEADS, GLA_DK, GLA_DV), F32)],
        compiler_params=_cparams(("arbitrary", "arbitrary")),
        name="gla",
    )(gq, gk, gv, gr, la, gain)


def _diff_kernel(far_ref, q_ref, k_ref, vt_ref, bias_ref, lq1_ref, lk1_ref, lq2_ref, lk2_ref,
                 sg_ref, o_ref, qs_ref, m_ref, acc_ref, s_ref, smax_ref, *, t, qc, lam_init):
    h = pl.program_id(1)
    i = pl.program_id(2)
    q = q_ref[...]
    lane = lax.broadcasted_iota(I32, q.shape, 1)
    zero = jnp.zeros_like(q)
    qs_ref[0:t, :] = jnp.where(lane < DIFF_DH, q, zero)
    qs_ref[t:2 * t, :] = jnp.where(lane >= DIFF_DH, q, zero)
    m_ref[...] = jnp.full(m_ref.shape, NEG_BIG, F32)
    acc_ref[...] = jnp.zeros(acc_ref.shape, F32)
    ones_rows = jnp.ones((ONES_ROWS, t), BF16)

    nc = 2 * t // qc
    far_bias = far_ref[h]

    def chunk(c):
        return slice(c * qc, (c + 1) * qc)

    def keys(j):
        return k_ref[pl.ds(pl.multiple_of(j * t, t), t), :]

    def values(j):
        return jnp.concatenate([vt_ref[j], ones_rows], axis=0)

    def scores(kj, c):
        return _dot_nt(kj, qs_ref[chunk(c), :])

    def update(c, s, smax, shift, vtj):
        m_prev = m_ref[:, chunk(c)]
        m_new = jnp.maximum(m_prev, smax + shift)
        alpha = jnp.exp2(m_prev - m_new)
        p = jnp.exp2((s - (m_new - shift)).astype(BF16))
        acc_ref[:, chunk(c)] = alpha * acc_ref[:, chunk(c)] + _dot(vtj, p)
        m_ref[:, chunk(c)] = m_new

    def near_steps(blocks):
        items = []
        for slot, (j, tile_idx) in enumerate(blocks):
            kj, vtj = keys(j), values(j)
            for c in range(nc):
                c0 = (c * qc) % t
                n_keys = c0 + qc if tile_idx == 0 else t
                items.append((c, c0, n_keys, tile_idx, slot, kj, vtj))

        def stage(item):
            c, c0, n_keys, tile_idx, slot, kj, _ = item
            s = scores(kj[:n_keys], c) + bias_ref[tile_idx, :n_keys, c0:c0 + qc]
            s_ref[slot, :n_keys, chunk(c)] = s
            smax_ref[slot, :, chunk(c)] = jnp.max(s, axis=0, keepdims=True)

        stage(items[0])
        for n, (c, _, n_keys, _, slot, _, vtj) in enumerate(items):
            if n + 1 < len(items):
                stage(items[n + 1])
            update(c, s_ref[slot, :n_keys, chunk(c)], smax_ref[slot, :, chunk(c)], 0.0,
                   vtj[:, :n_keys])

    def far_scores(kj, c, slot):
        s = scores(kj, c)
        s_ref[slot, :, chunk(c)] = s
        smax_ref[slot, :, chunk(c)] = jnp.max(s, axis=0, keepdims=True)

    def far_update(c, slot, vtj):
        update(c, s_ref[slot, :, chunk(c)], smax_ref[slot, :, chunk(c)], far_bias, vtj)

    n_far = i - 1

    def far_phase(j, slot, prefetch):
        vtj = values(j)
        if prefetch:
            k_next = keys(j + 1)
        for c in range(nc):
            if prefetch:
                far_scores(k_next, c, 1 - slot)
            far_update(c, slot, vtj)

    @pl.when(n_far > 0)
    def _():
        k0 = keys(0)
        for c in range(nc):
            far_scores(k0, c, 0)

        n_pairs = (n_far - 1) // 2

        def body(pair, carry):
            far_phase(2 * pair, 0, True)
            far_phase(2 * pair + 1, 1, True)
            return carry

        lax.fori_loop(0, n_pairs, body, 0)
        done = 2 * n_pairs

        @pl.when(n_far - done == 1)
        def _():
            far_phase(done, 0, False)

        @pl.when(n_far - done == 2)
        def _():
            far_phase(done, 0, True)
            far_phase(done + 1, 1, False)

    @pl.when(i > 0)
    def _():
        near_steps([(i - 1, 1), (i, 0)])

    @pl.when(i == 0)
    def _():
        near_steps([(i, 0)])

    lam = (jnp.exp(jnp.sum(lq1_ref[...] * lk1_ref[...], axis=-1, keepdims=True))
           - jnp.exp(jnp.sum(lq2_ref[...] * lk2_ref[...], axis=-1, keepdims=True)) + lam_init)
    out = acc_ref[0:DIFF_DV, :] / acc_ref[DIFF_DV:DIFF_DV + 1, :]
    o = out[:, :t] - lam * out[:, t:]
    on = o * lax.rsqrt(jnp.mean(o * o, axis=0, keepdims=True) + EPS) * sg_ref[...]
    o_ref[...] = (on * (1.0 - lam_init)).T.astype(o_ref.dtype)


def _diff(far_bias, dq, dk, dvt, bias_tiles, lq1, lk1, lq2, lk2, sg, B, S, t, lam_init):
    nq = S // t
    vec = lambda b, h, i: (0, 0)
    return pl.pallas_call(
        functools.partial(_diff_kernel, t=t, qc=min(t, 256), lam_init=lam_init),
        grid=(B, DIFF_HEADS, nq),
        in_specs=[
            pl.BlockSpec(memory_space=pltpu.SMEM),
            pl.BlockSpec((None, t, 128), lambda b, h, i: (b, i, h)),
            pl.BlockSpec((None, S, 128), lambda b, h, i: (b, 0, h)),
            pl.BlockSpec((nq, DIFF_DV, t), lambda b, h, i: (b, h, 0)),
            pl.BlockSpec((None, 2, t, t), lambda b, h, i: (h, 0, 0, 0)),
            pl.BlockSpec((1, DIFF_DH), vec),
            pl.BlockSpec((1, DIFF_DH), vec),
            pl.BlockSpec((1, DIFF_DH), vec),
            pl.BlockSpec((1, DIFF_DH), vec),
            pl.BlockSpec((DIFF_DV, 1), vec),
        ],
        out_specs=pl.BlockSpec((None, t, 128), lambda b, h, i: (b, i, h)),
        out_shape=jax.ShapeDtypeStruct((B, S, 512), BF16),
        scratch_shapes=[pltpu.VMEM((2 * t, 128), BF16), pltpu.VMEM((1, 2 * t), F32),
                        pltpu.VMEM((DIFF_DV + ONES_ROWS, 2 * t), F32),
                        pltpu.VMEM((2, t, 2 * t), F32), pltpu.VMEM((2, 1, 2 * t), F32)],
        compiler_params=_cparams(("arbitrary", "arbitrary", "arbitrary")),
        name="diff_attn",
    )(far_bias, dq, dk, dvt, bias_tiles, lq1, lk1, lq2, lk2, sg)


def _t5_causal_bucket(n):
    max_exact = REL_BUCKETS // 2
    nf = jnp.maximum(n, 1).astype(F32)
    large = max_exact + (jnp.log(nf / max_exact) / math.log(REL_MAX_DIST / max_exact)
                         * (REL_BUCKETS - max_exact)).astype(I32)
    large = jnp.minimum(large, REL_BUCKETS - 1)
    return jnp.where(n < max_exact, n, large)


def _bias_tiles(rel_bias, t):
    kj = jnp.arange(t)[:, None]
    qi = jnp.arange(t)[None, :]
    d0 = qi - kj

    def lookup(bucket):
        hot = bucket[None, :, :, None] == jnp.arange(REL_BUCKETS)
        return jnp.sum(jnp.where(hot, rel_bias.T[:, None, None, :], 0.0), axis=-1)

    diag = jnp.where(d0 >= 0, lookup(_t5_causal_bucket(jnp.maximum(d0, 0))), NEG_BIG)
    sub = lookup(_t5_causal_bucket(d0 + t))
    return jnp.stack([diag, sub], axis=1).astype(F32)


def _mix_kernel(x_ref, yg_ref, yd_ref, wo_ref, g2_ref, wr_ref, br_ref,
                h_ref, m_ref, e_ref, gate_ref, rank_ref, cnt_ref, base_ref, *, tm):
    @pl.when(pl.program_id(0) == 0)
    def _():
        base_ref[...] = jnp.zeros(base_ref.shape, F32)

    h1 = x_ref[...] + _dot(yg_ref[...], wo_ref[0:512, :]) + _dot(yd_ref[...], wo_ref[512:1024, :])
    h_ref[...] = h1
    ms = jnp.mean(h1 * h1, axis=-1, keepdims=True)
    m = h1 * lax.rsqrt(ms + EPS) * g2_ref[...]
    _store_slabs(m_ref, m)

    m_hi, m_lo = _split_bf16(m)
    w_hi, w_lo = _split_bf16(wr_ref[...])
    logits = _dot_nt(w_hi, m_hi) + _dot_nt(w_hi, m_lo) + _dot_nt(w_lo, m_hi) + br_ref[...]

    eidx = lax.broadcasted_iota(I32, logits.shape, 0)
    work = logits
    vals, idxs = [], []
    for _ in range(TOP_K):
        mx = jnp.max(work, axis=0, keepdims=True)
        ix = jnp.min(jnp.where(work == mx, eidx, N_EXPERTS), axis=0, keepdims=True)
        vals.append(mx)
        idxs.append(ix)
        work = jnp.where(eidx == ix, -jnp.inf, work)
    ex = [jnp.exp(v - vals[0]) for v in vals]
    den = ex[0] + ex[1] + ex[2] + ex[3]
    gate_ref[...] = jnp.concatenate([e / den for e in ex], axis=0)
    e_ref[...] = jnp.concatenate(idxs, axis=0)

    onehots = [jnp.where(eidx == ix, 1.0, 0.0) for ix in idxs]
    cnt = onehots[0] + onehots[1] + onehots[2] + onehots[3]
    ti = lax.broadcasted_iota(I32, (tm, tm), 0)
    tj = lax.broadcasted_iota(I32, (tm, tm), 1)
    upper = jnp.where(ti < tj, 1.0, 0.0).astype(BF16)
    prefix = _dot(cnt.astype(BF16), upper) + base_ref[...]
    rank_ref[...] = jnp.concatenate(
        [jnp.sum(oh * prefix, axis=0, keepdims=True) for oh in onehots], axis=0).astype(I32)
    base_ref[...] = base_ref[...] + jnp.sum(cnt, axis=1, keepdims=True)
    cnt_ref[...] = base_ref[...]


def _mix(x2, yg, yd, wo, g2, wr_t, br, tm):
    T, D = x2.shape
    row = lambda i: (i, 0)
    col = lambda i: (0, i)
    const = lambda i: (0, 0)
    return pl.pallas_call(
        functools.partial(_mix_kernel, tm=tm),
        grid=(T // tm,),
        in_specs=[
            pl.BlockSpec((tm, D), row),
            pl.BlockSpec((tm, 512), row),
            pl.BlockSpec((tm, 512), row),
            pl.BlockSpec(wo.shape, const),
            pl.BlockSpec((1, D), const),
            pl.BlockSpec(wr_t.shape, const),
            pl.BlockSpec(br.shape, const),
        ],
        out_specs=[
            pl.BlockSpec((tm, D), row),
            pl.BlockSpec((tm * SLAB, 128), row),
            pl.BlockSpec((TOP_K, tm), col),
            pl.BlockSpec((TOP_K, tm), col),
            pl.BlockSpec((TOP_K, tm), col),
            pl.BlockSpec((N_EXPERTS, 1), const),
        ],
        out_shape=[
            jax.ShapeDtypeStruct((T, D), F32),
            jax.ShapeDtypeStruct((T * SLAB, 128), F32),
            jax.ShapeDtypeStruct((TOP_K, T), I32),
            jax.ShapeDtypeStruct((TOP_K, T), F32),
            jax.ShapeDtypeStruct((TOP_K, T), I32),
            jax.ShapeDtypeStruct((N_EXPERTS, 1), F32),
        ],
        scratch_shapes=[pltpu.VMEM((N_EXPERTS, 1), F32)],
        compiler_params=_cparams(("arbitrary",)),
        name="mix_router",
    )(x2, yg, yd, wo, g2, wr_t, br)


def _slab(ref, row):
    return ref.at[pl.ds(pl.multiple_of(row * SLAB, SLAB), SLAB)]


def _slab_wait_all(src_ref, dst_ref, sem, n):
    def body(r, c):
        pltpu.make_async_copy(_slab(src_ref, 0), _slab(dst_ref, 0), sem).wait()
        return c
    lax.fori_loop(0, n, body, 0, unroll=8)


def _dispatch_kernel(pend_ref, padded_ref, dest_hbm, m_ref, xs_hbm, dest_smem, zero_ref,
                     sem_idx, sem_zero, sem, *, tm):
    i = pl.program_id(0)

    @pl.when(i == 0)
    def _():
        zero_ref[...] = jnp.zeros(zero_ref.shape, F32)

        def last_block(e):
            return xs_hbm.at[pl.ds(pl.multiple_of((pend_ref[e] - MOE_BLOCK) * SLAB, SLAB),
                                   MOE_BLOCK * SLAB)]

        def start(e, c):
            @pl.when(padded_ref[e] > 0)
            def _():
                pltpu.make_async_copy(zero_ref, last_block(e), sem_zero).start()
            return c

        def wait(e, c):
            @pl.when(padded_ref[e] > 0)
            def _():
                pltpu.make_async_copy(zero_ref, last_block(e), sem_zero).wait()
            return c

        lax.fori_loop(0, N_EXPERTS, start, 0)
        lax.fori_loop(0, N_EXPERTS, wait, 0)

    cp = pltpu.make_async_copy(dest_hbm.at[i], dest_smem, sem_idx)
    cp.start()
    cp.wait()

    def body(r, c):
        for k in range(TOP_K):
            d = dest_smem[k * tm + r]
            pltpu.make_async_copy(_slab(m_ref, r), _slab(xs_hbm, d), sem).start(priority=k % 2)
        return c
    lax.fori_loop(0, tm, body, 0, unroll=4)
    _slab_wait_all(m_ref, xs_hbm, sem, TOP_K * tm)


def _dispatch(pad_end, padded, dest_tiles, m_slabs, P, tm):
    T = m_slabs.shape[0] // SLAB
    return pl.pallas_call(
        functools.partial(_dispatch_kernel, tm=tm),
        grid_spec=pltpu.PrefetchScalarGridSpec(
            num_scalar_prefetch=2,
            grid=(T // tm,),
            in_specs=[
                pl.BlockSpec(memory_space=pl.ANY),
                pl.BlockSpec((tm * SLAB, 128), lambda i, pe, pd: (i, 0)),
            ],
            out_specs=pl.BlockSpec(memory_space=pl.ANY),
            scratch_shapes=[pltpu.SMEM((TOP_K * tm,), I32),
                            pltpu.VMEM((MOE_BLOCK * SLAB, 128), F32),
                            pltpu.SemaphoreType.DMA, pltpu.SemaphoreType.DMA,
                            pltpu.SemaphoreType.DMA],
        ),
        out_shape=jax.ShapeDtypeStruct((P * SLAB, 128), F32),
        compiler_params=_cparams(("arbitrary",)),
        name="moe_dispatch",
    )(pad_end, padded, dest_tiles, m_slabs)


def _experts_kernel(bexp_ref, nused_ref, xs_ref, wgu_ref, bgu_ref, wd_ref, bd_ref, ys_ref, *, dff):
    del bexp_ref

    @pl.when(pl.program_id(0) < nused_ref[0])
    def _():
        x = _load_slabs(xs_ref, MOE_BLOCK).astype(BF16)
        gu = _dot(x, wgu_ref[...]) + bgu_ref[...]
        glu = jnp.minimum(gu[:, :dff], SWIGLU_LIMIT)
        lin = jnp.clip(gu[:, dff:], -SWIGLU_LIMIT, SWIGLU_LIMIT)
        hid = glu * jax.nn.sigmoid(SWIGLU_ALPHA * glu) * (lin + 1.0)
        _store_slabs(ys_ref, _dot(hid.astype(BF16), wd_ref[...]) + bd_ref[...])


def _experts(block_exp, n_used, xs, wgu, bgu, wd, bd):
    D, dff = wd.shape[2], wd.shape[1]
    assert D == SLAB * 128
    P = xs.shape[0] // SLAB
    nb = P // MOE_BLOCK
    blk = lambda i, be, nu: (jnp.minimum(i, nu[0] - 1), 0)
    exp3 = lambda i, be, nu: (be[i], 0, 0)
    return pl.pallas_call(
        functools.partial(_experts_kernel, dff=dff),
        grid_spec=pltpu.PrefetchScalarGridSpec(
            num_scalar_prefetch=2,
            grid=(nb,),
            in_specs=[
                pl.BlockSpec((MOE_BLOCK * SLAB, 128), blk),
                pl.BlockSpec((None, D, 2 * dff), exp3),
                pl.BlockSpec((None, 1, 2 * dff), exp3),
                pl.BlockSpec((None, dff, D), exp3),
                pl.BlockSpec((None, 1, D), exp3),
            ],
            out_specs=pl.BlockSpec((MOE_BLOCK * SLAB, 128), blk),
        ),
        out_shape=jax.ShapeDtypeStruct((P * SLAB, 128), F32),
        compiler_params=_cparams(("arbitrary",)),
        name="moe_experts",
    )(block_exp, n_used, xs, wgu, bgu, wd, bd)


def _combine_kernel(dest_hbm, h_ref, gate_ref, ys_hbm, o_ref, buf, dest_smem, sem_idx, sem,
                    *, tm, nt):
    i = pl.program_id(0)
    slot = lax.rem(i, 2)

    def fetch(tile, slot):
        cp = pltpu.make_async_copy(dest_hbm.at[tile], dest_smem, sem_idx)
        cp.start()
        cp.wait()

        def body(r, c):
            for k in range(TOP_K):
                d = dest_smem[k * tm + r]
                pltpu.make_async_copy(_slab(ys_hbm, d), _slab(buf.at[slot, k], r),
                                      sem.at[slot]).start(priority=k % 2)
            return c
        lax.fori_loop(0, tm, body, 0, unroll=4)

    @pl.when(i == 0)
    def _():
        fetch(0, 0)

    @pl.when(i + 1 < nt)
    def _():
        fetch(i + 1, 1 - slot)

    _slab_wait_all(ys_hbm, buf.at[slot, 0], sem.at[slot], TOP_K * tm)

    g = gate_ref[...]
    acc = h_ref[...]
    for k in range(TOP_K):
        acc = acc + g[:, k:k + 1] * _load_slabs(buf.at[slot, k], tm)
    o_ref[...] = acc


def _combine(dest_tiles, h1, gates_t, ys, tm):
    T, D = h1.shape
    nt = T // tm
    return pl.pallas_call(
        functools.partial(_combine_kernel, tm=tm, nt=nt),
        grid=(nt,),
        in_specs=[
            pl.BlockSpec(memory_space=pl.ANY),
            pl.BlockSpec((tm, D), lambda i: (i, 0)),
            pl.BlockSpec((tm, TOP_K), lambda i: (i, 0)),
            pl.BlockSpec(memory_space=pl.ANY),
        ],
        out_specs=pl.BlockSpec((tm, D), lambda i: (i, 0)),
        out_shape=jax.ShapeDtypeStruct((T, D), F32),
        scratch_shapes=[pltpu.VMEM((2, TOP_K, tm * SLAB, 128), F32),
                        pltpu.SMEM((TOP_K * tm,), I32),
                        pltpu.SemaphoreType.DMA, pltpu.SemaphoreType.DMA((2,))],
        compiler_params=_cparams(("arbitrary",)),
        name="moe_combine",
    )(dest_tiles, h1, gates_t, ys)


def _tile(n, pref):
    t = min(n, pref)
    assert n % t == 0, (n, pref)
    return t


def _layer(h, l, p, rel_bias):
    B, S, D = h.shape
    T = B * S
    x2 = h.reshape(T, D)
    lam_init = 0.8 - 0.6 * math.exp(-0.3 * l)

    w_in = p['w_in']
    c_low = 1024
    wm = jnp.concatenate([w_in[:, :c_low], w_in[:, c_low + GLA_RANK:]], axis=1).astype(BF16)
    wlow = jnp.pad(w_in[:, c_low:c_low + GLA_RANK], ((0, 0), (0, 128 - GLA_RANK)))
    wg2 = jnp.pad(p['w_gla_gate2'], ((0, 128 - GLA_RANK), (0, 0)))
    bg2 = p['b_gla_gate2'].reshape(1, -1)
    gidx = jnp.arange(512) // DIFF_DH
    bd = (gidx[:, None] == gidx[None, :]).astype(BF16)
    qg = jnp.tile(p['diff_q_gain'], 512 // DIFF_DH).reshape(1, 512)
    kg = jnp.tile(p['diff_k_gain'], 512 // DIFF_DH).reshape(1, 512)

    tm = _tile(T, 512)
    t = _tile(S, 512)
    assert t >= 128
    gq, gk, gv, gr, la, dq, dk, dvt = _proj(
        x2, p['norm1_gain'].reshape(1, D), wm, wlow, wg2, bg2, bd, qg, kg, tm, t)

    y_gla = _gla(gq, gk, gv, gr, la, p['gla_out_gain'].reshape(1, GLA_DV), B, S, _tile(S, 256))

    rel_bias2 = rel_bias.astype(F32) * LOG2E
    y_diff = _diff(rel_bias2[REL_BUCKETS - 1],
                   dq.reshape(B, S, 512), dk.reshape(B, S, 512), dvt,
                   _bias_tiles(rel_bias2, t),
                   p['lam_q1'].reshape(1, -1), p['lam_k1'].reshape(1, -1),
                   p['lam_q2'].reshape(1, -1), p['lam_k2'].reshape(1, -1),
                   p['diff_sub_gain'].reshape(-1, 1), B, S, t, lam_init).reshape(T, 512)

    h1, m, e_t, gate_t, rank_t, counts = _mix(
        x2, y_gla, y_diff, p['w_out'].astype(BF16), p['norm2_gain'].reshape(1, D),
        p['w_router'].T, p['b_router'].reshape(-1, 1), tm)

    counts = counts.reshape(-1).astype(I32)
    padded = (counts + MOE_BLOCK - 1) // MOE_BLOCK * MOE_BLOCK
    pad_end = jnp.cumsum(padded)
    pad_start = pad_end - padded
    n_blocks = -(-(T * TOP_K) // MOE_BLOCK) + N_EXPERTS
    P = n_blocks * MOE_BLOCK
    blk_start = jnp.arange(n_blocks, dtype=I32) * MOE_BLOCK
    block_exp = jnp.minimum(jnp.sum((pad_end[None, :] <= blk_start[:, None]).astype(I32), axis=1),
                            N_EXPERTS - 1)
    n_used = (pad_end[-1:] // MOE_BLOCK).astype(I32)
    eids = jnp.arange(N_EXPERTS, dtype=I32)
    dest = rank_t + jnp.sum(jnp.where(e_t[..., None] == eids, pad_start, 0), axis=-1)

    tmd = _tile(T, 256)
    dest_tiles = dest.reshape(TOP_K, T // tmd, tmd).transpose(1, 0, 2).reshape(T // tmd, TOP_K * tmd)

    xs = _dispatch(pad_end, padded, dest_tiles, m, P, tmd)
    ys = _experts(block_exp, n_used, xs, p['w_gate_up'].astype(BF16),
                  p['b_gate_up'].reshape(N_EXPERTS, 1, -1), p['w_down'].astype(BF16),
                  p['b_down'].reshape(N_EXPERTS, 1, -1))
    out = _combine(dest_tiles, h1, gate_t.T, ys, tmd)
    return out.reshape(B, S, D)


def kernel(x, norm1_gain, w_in, w_gla_gate2, b_gla_gate2, gla_out_gain, diff_q_gain, diff_k_gain,
           lam_q1, lam_k1, lam_q2, lam_k2, diff_sub_gain, rel_bias, w_out, norm2_gain,
           w_router, b_router, w_gate_up, b_gate_up, w_down, b_down):
    stacked = dict(norm1_gain=norm1_gain, w_in=w_in, w_gla_gate2=w_gla_gate2,
                   b_gla_gate2=b_gla_gate2, gla_out_gain=gla_out_gain, diff_q_gain=diff_q_gain,
                   diff_k_gain=diff_k_gain, lam_q1=lam_q1, lam_k1=lam_k1, lam_q2=lam_q2,
                   lam_k2=lam_k2, diff_sub_gain=diff_sub_gain, w_out=w_out, norm2_gain=norm2_gain,
                   w_router=w_router, b_router=b_router, w_gate_up=w_gate_up,
                   b_gate_up=b_gate_up, w_down=w_down, b_down=b_down)
    h = x
    for l in range(norm1_gain.shape[0]):
        h = _layer(h, l, {k: v[l] for k, v in stacked.items()}, rel_bias)
    return h
```

```python
import functools
import math

import jax
import jax.numpy as jnp
from jax import lax
from jax.experimental import pallas as pl
from jax.experimental.pallas import tpu as pltpu

F32 = jnp.float32
BF16 = jnp.bfloat16
I32 = jnp.int32

EPS = 1e-6
GLA_HEADS = 4
GLA_DK = 64
GLA_DV = 128
GLA_RANK = 16
GLA_GATE_NORM = 16.0
GLA_CHUNK = 64
DIFF_HEADS = 4
DIFF_DH = 64
DIFF_DV = 128
REL_BUCKETS = 32
REL_MAX_DIST = 128
N_EXPERTS = 32
TOP_K = 4
SWIGLU_LIMIT = 7.0
SWIGLU_ALPHA = 1.702
MOE_BLOCK = 512
NEG_BIG = -1e30
LOG2E = math.log2(math.e)
ONES_ROWS = 16
IDX_GROUP = 8

VMEM_LIMIT = 56 * 1024 * 1024


def _cparams(sem, flags=None):
    return pltpu.CompilerParams(dimension_semantics=sem, vmem_limit_bytes=VMEM_LIMIT, flags=flags)


def _dot(a, b):
    return jnp.dot(a, b, preferred_element_type=F32)


def _dot_nt(a, b):
    return lax.dot_general(a, b, (((1,), (1,)), ((), ())), preferred_element_type=F32)


def _dot_tn(a, b):
    return lax.dot_general(a, b, (((0,), (0,)), ((), ())), preferred_element_type=F32)


def _split_bf16(x):
    hi = x.astype(BF16)
    lo = (x - hi.astype(F32)).astype(BF16)
    return hi, lo


SLAB = 8


def _store_slabs(ref, x):
    n = x.shape[0]
    for s in range(SLAB):
        ref[pl.ds(s, n, stride=SLAB), :] = x[:, s * 128:(s + 1) * 128]


def _load_slabs(ref, n):
    return jnp.concatenate([ref[pl.ds(s, n, stride=SLAB), :] for s in range(SLAB)], axis=1)


def _proj_kernel(x_ref, g1_ref, wm_ref, wlow_ref, wg2_ref, bg2_ref, bd_ref, qg_ref, kg_ref,
                 gq_ref, gk_ref, gv_ref, gr_ref, la_ref, dq_ref, dk_ref, dvt_ref, wf_ref, *, tm, tv):
    @pl.when(pl.program_id(0) == 0)
    def _():
        wf_ref[...] = _dot(wlow_ref[...].astype(BF16), wg2_ref[...].astype(BF16)).astype(BF16)

    x = x_ref[...]
    ms = jnp.mean(x * x, axis=-1, keepdims=True)
    n = (x * lax.rsqrt(ms + EPS) * g1_ref[...]).astype(BF16)

    def mm(a, b):
        return _dot(n, wm_ref[:, a:b])

    gq_ref[...] = (mm(0, 256) * (GLA_DK ** -0.5)).astype(BF16)
    gk_ref[...] = mm(256, 512).astype(BF16)
    gv_ref[...] = mm(512, 1024).astype(BF16)
    gr_ref[...] = mm(1024, 1536).astype(BF16)

    z = _dot(n, wf_ref[...]) + bg2_ref[...]
    la_ref[...] = (jnp.minimum(z, 0.0) - jnp.log(1.0 + jnp.exp(-jnp.abs(z)))) * (1.0 / GLA_GATE_NORM)

    def qknorm(y, gain):
        ss = _dot((y * y).astype(BF16), bd_ref[...])
        return y * lax.rsqrt(ss * (1.0 / DIFF_DH) + EPS) * gain

    dq_ref[...] = (qknorm(mm(1536, 2048), qg_ref[...]) * (DIFF_DH ** -0.5 * LOG2E)).astype(BF16)
    dk_ref[...] = qknorm(mm(2048, 2560), kg_ref[...]).astype(BF16)
    dv = mm(2560, 3072)
    for u in range(tm // tv):
        dvt_ref[u] = dv[u * tv:(u + 1) * tv, :].T.astype(BF16)


def _proj(x2, g1, wm, wlow, wg2, bg2, bd, qg, kg, tm, tv):
    T, D = x2.shape
    const = lambda i: (0, 0)
    row = lambda i: (i, 0)
    outs = [(256, BF16), (256, BF16), (512, BF16), (512, BF16), (256, F32),
            (512, BF16), (512, BF16)]
    return pl.pallas_call(
        functools.partial(_proj_kernel, tm=tm, tv=tv),
        grid=(T // tm,),
        in_specs=[
            pl.BlockSpec((tm, D), row),
            pl.BlockSpec((1, D), const),
            pl.BlockSpec(wm.shape, const),
            pl.BlockSpec(wlow.shape, const),
            pl.BlockSpec(wg2.shape, const),
            pl.BlockSpec(bg2.shape, const),
            pl.BlockSpec(bd.shape, const),
            pl.BlockSpec(qg.shape, const),
            pl.BlockSpec(kg.shape, const),
        ],
        out_specs=[pl.BlockSpec((tm, w), row) for w, _ in outs]
        + [pl.BlockSpec((tm // tv, 512, tv), lambda i: (i, 0, 0))],
        out_shape=[jax.ShapeDtypeStruct((T, w), dt) for w, dt in outs]
        + [jax.ShapeDtypeStruct((T // tv, 512, tv), BF16)],
        scratch_shapes=[pltpu.VMEM((D, 256), BF16)],
        compiler_params=_cparams(("arbitrary",)),
        name="proj",
    )(x2, g1, wm, wlow, wg2, bg2, bd, qg, kg)


def _gla_kernel(q_ref, k_ref, v_ref, r_ref, la_ref, gain_ref, o_ref, state_ref, *, tg):
    C = GLA_CHUNK

    @pl.when(pl.program_id(1) == 0)
    def _():
        state_ref[...] = jnp.zeros(state_ref.shape, F32)

    nchunk = tg // C
    chunks = [slice(c * C, (c + 1) * C) for c in range(nchunk)]
    heads = [(slice(h * GLA_DK, (h + 1) * GLA_DK), slice(h * GLA_DV, (h + 1) * GLA_DV))
             for h in range(GLA_HEADS)]
    gain = gain_ref[...]

    ri = lax.broadcasted_iota(I32, (tg, tg), 0)
    ci = lax.broadcasted_iota(I32, (tg, tg), 1)
    shift = C.bit_length() - 1
    same_chunk = lax.shift_right_logical(ri, shift) == lax.shift_right_logical(ci, shift)
    tril = jnp.where(same_chunk & (ri >= ci), 1.0, 0.0).astype(BF16)
    causal = lax.broadcasted_iota(I32, (C, C), 0) >= lax.broadcasted_iota(I32, (C, C), 1)
    ones_cv = jnp.ones((C, GLA_DV), BF16)

    la_hi, la_lo = _split_bf16(la_ref[...])
    b = _dot(tril, la_hi) + _dot(tril, la_lo)
    ones_bd = jnp.where(same_chunk, 1.0, 0.0).astype(BF16)
    b_last = _dot(ones_bd, la_hi) + _dot(ones_bd, la_lo)
    q = q_ref[...].astype(F32)
    k = k_ref[...].astype(F32)
    q_dec = (q * jnp.exp(b)).astype(BF16)
    k_dec = (k * jnp.exp(-b)).astype(BF16)
    k_last = (k * jnp.exp(b_last - b)).astype(BF16)

    att = {(c, h): jnp.where(causal, _dot_nt(q_dec[sl, hs], k_dec[sl, hs]), 0.0).astype(BF16)
           for c, sl in enumerate(chunks) for h, (hs, _) in enumerate(heads)}
    o_intra = {(c, h): _dot(att[c, h], v_ref[sl, vs])
               for c, sl in enumerate(chunks) for h, (_, vs) in enumerate(heads)}
    kv = {(c, h): _dot_tn(k_last[sl, hs], v_ref[sl, vs])
          for c, sl in enumerate(chunks) for h, (hs, vs) in enumerate(heads)}
    decay = [jnp.exp(_dot_tn(la_hi[sl, :], ones_cv) + _dot_tn(la_lo[sl, :], ones_cv))
             for sl in chunks]

    states = [state_ref[h] for h in range(GLA_HEADS)]
    for c, sl in enumerate(chunks):
        for h, (hs, vs) in enumerate(heads):
            o = o_intra[c, h] + _dot(q_dec[sl, hs], states[h].astype(BF16))
            states[h] = states[h] * decay[c][hs, :] + kv[c, h]
            on = o * lax.rsqrt(jnp.mean(o * o, axis=-1, keepdims=True) + EPS) * gain
            r = r_ref[sl, vs].astype(F32)
            o_ref[sl, vs] = (on * (r * jax.nn.sigmoid(r))).astype(o_ref.dtype)
    for h in range(GLA_HEADS):
        state_ref[h] = states[h]


def _gla(gq, gk, gv, gr, la, gain, B, S, tg):
    nt = S // tg
    row = lambda b, t: (b * nt + t, 0)
    const = lambda b, t: (0, 0)
    return pl.pallas_call(
        functools.partial(_gla_kernel, tg=tg),
        grid=(B, nt),
        in_specs=[
            pl.BlockSpec((tg, 256), row),
            pl.BlockSpec((tg, 256), row),
            pl.BlockSpec((tg, 512), row),
            pl.BlockSpec((tg, 512), row),
            pl.BlockSpec((tg, 256), row),
            pl.BlockSpec((1, GLA_DV), const),
        ],
        out_specs=pl.BlockSpec((tg, 512), row),
        out_shape=jax.ShapeDtypeStruct((B * S, 512), BF16),
        scratch_shapes=[pltpu.VMEM((GLA_HEADS, GLA_DK, GLA_DV), F32)],
        compiler_params=_cparams(("arbitrary", "arbitrary")),
        name="gla",
    )(gq, gk, gv, gr, la, gain)


def _diff_kernel(far_ref, q_ref, k_ref, vt_ref, bias_ref, lq1_ref, lk1_ref, lq2_ref, lk2_ref,
                 sg_ref, o_ref, qs_ref, m_ref, acc_ref, s_ref, smax_ref, *, t, qc, lam_init):
    h = pl.program_id(1)
    i = pl.program_id(2)
    q = q_ref[...]
    lane = lax.broadcasted_iota(I32, q.shape, 1)
    zero = jnp.zeros_like(q)
    qs_ref[0:t, :] = jnp.where(lane < DIFF_DH, q, zero)
    qs_ref[t:2 * t, :] = jnp.where(lane >= DIFF_DH, q, zero)
    m_ref[...] = jnp.full(m_ref.shape, NEG_BIG, F32)
    acc_ref[...] = jnp.zeros(acc_ref.shape, F32)
    ones_rows = jnp.ones((ONES_ROWS, t), BF16)

    nc = 2 * t // qc
    far_bias = far_ref[h]

    def chunk(c):
        return slice(c * qc, (c + 1) * qc)

    def keys(j):
        return k_ref[pl.ds(pl.multiple_of(j * t, t), t), :]

    def values(j):
        return jnp.concatenate([vt_ref[j], ones_rows], axis=0)

    def scores(kj, c):
        return _dot_nt(kj, qs_ref[chunk(c), :])

    def update(c, s, smax, shift, vtj):
        m_prev = m_ref[:, chunk(c)]
        m_new = jnp.maximum(m_prev, smax + shift)
        alpha = jnp.exp2(m_prev - m_new)
        p = jnp.exp2((s - (m_new - shift)).astype(BF16))
        acc_ref[:, chunk(c)] = alpha * acc_ref[:, chunk(c)] + _dot(vtj, p)
        m_ref[:, chunk(c)] = m_new

    def near_steps(blocks):
        items = []
        for slot, (j, tile_idx) in enumerate(blocks):
            kj, vtj = keys(j), values(j)
            for c in range(nc):
                c0 = (c * qc) % t
                n_keys = c0 + qc if tile_idx == 0 else t
                items.append((c, c0, n_keys, tile_idx, slot, kj, vtj))

        def stage(item):
            c, c0, n_keys, tile_idx, slot, kj, _ = item
            s = scores(kj[:n_keys], c) + bias_ref[tile_idx, :n_keys, c0:c0 + qc]
            s_ref[slot, :n_keys, chunk(c)] = s
            smax_ref[slot, :, chunk(c)] = jnp.max(s, axis=0, keepdims=True)

        stage(items[0])
        for n, (c, _, n_keys, _, slot, _, vtj) in enumerate(items):
            if n + 1 < len(items):
                stage(items[n + 1])
            update(c, s_ref[slot, :n_keys, chunk(c)], smax_ref[slot, :, chunk(c)], 0.0,
                   vtj[:, :n_keys])

    def far_scores(kj, c, slot):
        s = scores(kj, c)
        s_ref[slot, :, chunk(c)] = s
        smax_ref[slot, :, chunk(c)] = jnp.max(s, axis=0, keepdims=True)

    def far_update(c, slot, vtj):
        update(c, s_ref[slot, :, chunk(c)], smax_ref[slot, :, chunk(c)], far_bias, vtj)

    n_far = i - 1

    def far_phase(j, slot, prefetch):
        vtj = values(j)
        if prefetch:
            k_next = keys(j + 1)
        for c in range(nc):
            if prefetch:
                far_scores(k_next, c, 1 - slot)
            far_update(c, slot, vtj)

    @pl.when(n_far > 0)
    def _():
        k0 = keys(0)
        for c in range(nc):
            far_scores(k0, c, 0)

        n_pairs = (n_far - 1) // 2

        def body(pair, carry):
            far_phase(2 * pair, 0, True)
            far_phase(2 * pair + 1, 1, True)
            return carry

        lax.fori_loop(0, n_pairs, body, 0)
        done = 2 * n_pairs

        @pl.when(n_far - done == 1)
        def _():
            far_phase(done, 0, False)

        @pl.when(n_far - done == 2)
        def _():
            far_phase(done, 0, True)
            far_phase(done + 1, 1, False)

    @pl.when(i > 0)
    def _():
        near_steps([(i - 1, 1), (i, 0)])

    @pl.when(i == 0)
    def _():
        near_steps([(i, 0)])

    lam = (jnp.exp(jnp.sum(lq1_ref[...] * lk1_ref[...], axis=-1, keepdims=True))
           - jnp.exp(jnp.sum(lq2_ref[...] * lk2_ref[...], axis=-1, keepdims=True)) + lam_init)
    out = acc_ref[0:DIFF_DV, :] / acc_ref[DIFF_DV:DIFF_DV + 1, :]
    o = out[:, :t] - lam * out[:, t:]
    on = o * lax.rsqrt(jnp.mean(o * o, axis=0, keepdims=True) + EPS) * sg_ref[...]
    o_ref[...] = (on * (1.0 - lam_init)).T.astype(o_ref.dtype)


def _diff(far_bias, dq, dk, dvt, bias_tiles, lq1, lk1, lq2, lk2, sg, B, S, t, lam_init):
    nq = S // t
    vec = lambda b, h, i: (0, 0)
    return pl.pallas_call(
        functools.partial(_diff_kernel, t=t, qc=min(t, 256), lam_init=lam_init),
        grid=(B, DIFF_HEADS, nq),
        in_specs=[
            pl.BlockSpec(memory_space=pltpu.SMEM),
            pl.BlockSpec((None, t, 128), lambda b, h, i: (b, i, h)),
            pl.BlockSpec((None, S, 128), lambda b, h, i: (b, 0, h)),
            pl.BlockSpec((nq, DIFF_DV, t), lambda b, h, i: (b, h, 0)),
            pl.BlockSpec((None, 2, t, t), lambda b, h, i: (h, 0, 0, 0)),
            pl.BlockSpec((1, DIFF_DH), vec),
            pl.BlockSpec((1, DIFF_DH), vec),
            pl.BlockSpec((1, DIFF_DH), vec),
            pl.BlockSpec((1, DIFF_DH), vec),
            pl.BlockSpec((DIFF_DV, 1), vec),
        ],
        out_specs=pl.BlockSpec((None, t, 128), lambda b, h, i: (b, i, h)),
        out_shape=jax.ShapeDtypeStruct((B, S, 512), BF16),
        scratch_shapes=[pltpu.VMEM((2 * t, 128), BF16), pltpu.VMEM((1, 2 * t), F32),
                        pltpu.VMEM((DIFF_DV + ONES_ROWS, 2 * t), F32),
                        pltpu.VMEM((2, t, 2 * t), F32), pltpu.VMEM((2, 1, 2 * t), F32)],
        compiler_params=_cparams(("arbitrary", "arbitrary", "arbitrary")),
        name="diff_attn",
    )(far_bias, dq, dk, dvt, bias_tiles, lq1, lk1, lq2, lk2, sg)


def _t5_causal_bucket(n):
    max_exact = REL_BUCKETS // 2
    nf = jnp.maximum(n, 1).astype(F32)
    large = max_exact + (jnp.log(nf / max_exact) / math.log(REL_MAX_DIST / max_exact)
                         * (REL_BUCKETS - max_exact)).astype(I32)
    large = jnp.minimum(large, REL_BUCKETS - 1)
    return jnp.where(n < max_exact, n, large)


def _bias_tiles(rel_bias, t):
    hot = _t5_causal_bucket(jnp.arange(2 * t))[:, None] == jnp.arange(REL_BUCKETS)
    f = jnp.sum(jnp.where(hot[None], rel_bias.T[:, None, :], 0.0), axis=-1).astype(F32)
    n_heads = f.shape[0]

    def toeplitz(w):
        rep = jnp.tile(w, (1, t))[:, :t * 2 * t].reshape(n_heads, t, 2 * t)
        return rep[:, :, :t]

    gap = jnp.zeros((n_heads, 2), F32)
    masked = jnp.full((n_heads, t - 1), NEG_BIG, F32)
    diag = toeplitz(jnp.concatenate([f[:, :t], gap, masked], axis=1))
    sub = toeplitz(jnp.concatenate([f[:, t:], gap, f[:, 1:t]], axis=1))
    return jnp.stack([diag, sub], axis=1)


def _mix_kernel(x_ref, yg_ref, yd_ref, wo_ref, g2_ref, wr_ref, br_ref,
                h_ref, m_ref, e_ref, gate_ref, rank_ref, cnt_ref, base_ref, *, tm):
    @pl.when(pl.program_id(0) == 0)
    def _():
        base_ref[...] = jnp.zeros(base_ref.shape, F32)

    h1 = x_ref[...] + _dot(yg_ref[...], wo_ref[0:512, :]) + _dot(yd_ref[...], wo_ref[512:1024, :])
    h_ref[...] = h1
    ms = jnp.mean(h1 * h1, axis=-1, keepdims=True)
    m = h1 * lax.rsqrt(ms + EPS) * g2_ref[...]
    _store_slabs(m_ref, m)

    m_hi, m_lo = _split_bf16(m)
    w_hi, w_lo = _split_bf16(wr_ref[...])
    logits = _dot_nt(w_hi, m_hi) + _dot_nt(w_hi, m_lo) + _dot_nt(w_lo, m_hi) + br_ref[...]

    eidx = lax.broadcasted_iota(I32, logits.shape, 0)
    work = logits
    vals, idxs = [], []
    for _ in range(TOP_K):
        mx = jnp.max(work, axis=0, keepdims=True)
        ix = jnp.min(jnp.where(work == mx, eidx, N_EXPERTS), axis=0, keepdims=True)
        vals.append(mx)
        idxs.append(ix)
        work = jnp.where(eidx == ix, -jnp.inf, work)
    ex = [jnp.exp(v - vals[0]) for v in vals]
    den = ex[0] + ex[1] + ex[2] + ex[3]
    gate_ref[...] = jnp.concatenate([e / den for e in ex], axis=0)
    e_ref[...] = jnp.concatenate(idxs, axis=0)

    onehots = [jnp.where(eidx == ix, 1.0, 0.0) for ix in idxs]
    cnt = onehots[0] + onehots[1] + onehots[2] + onehots[3]
    ti = lax.broadcasted_iota(I32, (tm, tm), 0)
    tj = lax.broadcasted_iota(I32, (tm, tm), 1)
    upper = jnp.where(ti < tj, 1.0, 0.0).astype(BF16)
    prefix = _dot(cnt.astype(BF16), upper) + base_ref[...]
    rank_ref[...] = jnp.concatenate(
        [jnp.sum(oh * prefix, axis=0, keepdims=True) for oh in onehots], axis=0).astype(I32)
    base_ref[...] = base_ref[...] + jnp.sum(cnt, axis=1, keepdims=True)
    cnt_ref[...] = base_ref[...]


def _mix(x2, yg, yd, wo, g2, wr_t, br, tm):
    T, D = x2.shape
    row = lambda i: (i, 0)
    col = lambda i: (0, i)
    const = lambda i: (0, 0)
    return pl.pallas_call(
        functools.partial(_mix_kernel, tm=tm),
        grid=(T // tm,),
        in_specs=[
            pl.BlockSpec((tm, D), row),
            pl.BlockSpec((tm, 512), row),
            pl.BlockSpec((tm, 512), row),
            pl.BlockSpec(wo.shape, const),
            pl.BlockSpec((1, D), const),
            pl.BlockSpec(wr_t.shape, const),
            pl.BlockSpec(br.shape, const),
        ],
        out_specs=[
            pl.BlockSpec((tm, D), row),
            pl.BlockSpec((tm * SLAB, 128), row),
            pl.BlockSpec((TOP_K, tm), col),
            pl.BlockSpec((TOP_K, tm), col),
            pl.BlockSpec((TOP_K, tm), col),
            pl.BlockSpec((N_EXPERTS, 1), const),
        ],
        out_shape=[
            jax.ShapeDtypeStruct((T, D), F32),
            jax.ShapeDtypeStruct((T * SLAB, 128), F32),
            jax.ShapeDtypeStruct((TOP_K, T), I32),
            jax.ShapeDtypeStruct((TOP_K, T), F32),
            jax.ShapeDtypeStruct((TOP_K, T), I32),
            jax.ShapeDtypeStruct((N_EXPERTS, 1), F32),
        ],
        scratch_shapes=[pltpu.VMEM((N_EXPERTS, 1), F32)],
        compiler_params=_cparams(("arbitrary",)),
        name="mix_router",
    )(x2, yg, yd, wo, g2, wr_t, br)


def _slab(ref, row):
    return ref.at[pl.ds(pl.multiple_of(row * SLAB, SLAB), SLAB)]


def _slab_wait_all(src_ref, dst_ref, sem, n):
    def body(r, c):
        pltpu.make_async_copy(_slab(src_ref, 0), _slab(dst_ref, 0), sem).wait()
        return c
    lax.fori_loop(0, n, body, 0, unroll=8)


def _index_group_copy(dest_hbm, idx, sem_idx, group):
    return pltpu.make_async_copy(dest_hbm.at[group], idx.at[lax.rem(group, 2)], sem_idx)


def _dispatch_kernel(pend_ref, padded_ref, dest_hbm, m_hbm, xs_hbm, mbuf, idx, zero_ref,
                     sem_idx, sem_zero, sem_ld, sem_sc, *, tm, nt, grp):
    i = pl.program_id(0)
    n_idx = TOP_K * tm

    def load(tile):
        slot = lax.rem(tile, 3)
        rows = pl.ds(pl.multiple_of(tile * (tm * SLAB), tm * SLAB), tm * SLAB)
        return pltpu.make_async_copy(m_hbm.at[rows], mbuf.at[slot], sem_ld.at[slot])

    def wait_scatters(tile):
        _slab_wait_all(mbuf.at[0], xs_hbm, sem_sc.at[lax.rem(tile, 3)], n_idx)

    @pl.when(i == 0)
    def _():
        zero_ref[...] = jnp.zeros(zero_ref.shape, F32)

        def last_block(e):
            return xs_hbm.at[pl.ds(pl.multiple_of((pend_ref[e] - MOE_BLOCK) * SLAB, SLAB),
                                   MOE_BLOCK * SLAB)]

        def start(e, c):
            @pl.when(padded_ref[e] > 0)
            def _():
                pltpu.make_async_copy(zero_ref, last_block(e), sem_zero).start()
            return c

        def wait(e, c):
            @pl.when(padded_ref[e] > 0)
            def _():
                pltpu.make_async_copy(zero_ref, last_block(e), sem_zero).wait()
            return c

        lax.fori_loop(0, N_EXPERTS, start, 0)
        lax.fori_loop(0, N_EXPERTS, wait, 0)

        first = _index_group_copy(dest_hbm, idx, sem_idx, 0)
        first.start()
        first.wait()
        load(0).start()

    @pl.when((i > 0) & (lax.rem(i, grp) == 0))
    def _():
        _index_group_copy(dest_hbm, idx, sem_idx, i // grp).wait()

    @pl.when(i >= 2)
    def _():
        wait_scatters(i - 2)

    @pl.when(i + 1 < nt)
    def _():
        load(i + 1).start()

    @pl.when((i + 1 < nt) & (lax.rem(i + 1, grp) == 0))
    def _():
        _index_group_copy(dest_hbm, idx, sem_idx, (i + 1) // grp).start()

    load(i).wait()
    slot = lax.rem(i, 3)
    gslot = lax.rem(i // grp, 2)
    off = lax.rem(i, grp) * n_idx

    def body(r, c):
        for k in range(TOP_K):
            d = idx[gslot, off + k * tm + r]
            pltpu.make_async_copy(_slab(mbuf.at[slot], r), _slab(xs_hbm, d),
                                  sem_sc.at[slot]).start(priority=k % 2)
        return c
    lax.fori_loop(0, tm, body, 0, unroll=4)

    @pl.when(i == nt - 1)
    def _():
        if nt > 1:
            wait_scatters(i - 1)
        wait_scatters(i)


def _dispatch(pad_end, padded, dest_groups, m_slabs, P, tm, grp):
    T = m_slabs.shape[0] // SLAB
    nt = T // tm
    return pl.pallas_call(
        functools.partial(_dispatch_kernel, tm=tm, nt=nt, grp=grp),
        grid_spec=pltpu.PrefetchScalarGridSpec(
            num_scalar_prefetch=2,
            grid=(nt,),
            in_specs=[
                pl.BlockSpec(memory_space=pl.ANY),
                pl.BlockSpec(memory_space=pl.ANY),
            ],
            out_specs=pl.BlockSpec(memory_space=pl.ANY),
            scratch_shapes=[pltpu.VMEM((3, tm * SLAB, 128), F32),
                            pltpu.SMEM((2, grp * TOP_K * tm), I32),
                            pltpu.VMEM((MOE_BLOCK * SLAB, 128), F32),
                            pltpu.SemaphoreType.DMA, pltpu.SemaphoreType.DMA,
                            pltpu.SemaphoreType.DMA((3,)), pltpu.SemaphoreType.DMA((3,))],
        ),
        out_shape=jax.ShapeDtypeStruct((P * SLAB, 128), F32),
        compiler_params=_cparams(("arbitrary",)),
        name="moe_dispatch",
    )(pad_end, padded, dest_groups, m_slabs)


def _experts_kernel(bexp_ref, nused_ref, xs_ref, wgu_ref, bgu_ref, wd_ref, bd_ref, ys_ref,
                    wgu_bf, wd_bf, *, dff):
    i = pl.program_id(0)

    @pl.when(i < nused_ref[0])
    def _():
        @pl.when((i == 0) | (bexp_ref[i] != bexp_ref[jnp.maximum(i - 1, 0)]))
        def _():
            wgu_bf[...] = wgu_ref[...].astype(BF16)
            wd_bf[...] = wd_ref[...].astype(BF16)

        x = _load_slabs(xs_ref, MOE_BLOCK).astype(BF16)
        gu = _dot(x, wgu_bf[...]) + bgu_ref[...]
        glu = jnp.minimum(gu[:, :dff], SWIGLU_LIMIT)
        lin = jnp.clip(gu[:, dff:], -SWIGLU_LIMIT, SWIGLU_LIMIT)
        hid = glu * jax.nn.sigmoid(SWIGLU_ALPHA * glu) * (lin + 1.0)
        _store_slabs(ys_ref, _dot(hid.astype(BF16), wd_bf[...]) + bd_ref[...])


def _experts(block_exp, n_used, xs, wgu, bgu, wd, bd):
    D, dff = wd.shape[2], wd.shape[1]
    assert D == SLAB * 128
    P = xs.shape[0] // SLAB
    nb = P // MOE_BLOCK
    blk = lambda i, be, nu: (jnp.minimum(i, nu[0] - 1), 0)
    exp3 = lambda i, be, nu: (be[i], 0, 0)
    return pl.pallas_call(
        functools.partial(_experts_kernel, dff=dff),
        grid_spec=pltpu.PrefetchScalarGridSpec(
            num_scalar_prefetch=2,
            grid=(nb,),
            in_specs=[
                pl.BlockSpec((MOE_BLOCK * SLAB, 128), blk),
                pl.BlockSpec((None, D, 2 * dff), exp3),
                pl.BlockSpec((None, 1, 2 * dff), exp3),
                pl.BlockSpec((None, dff, D), exp3),
                pl.BlockSpec((None, 1, D), exp3),
            ],
            out_specs=pl.BlockSpec((MOE_BLOCK * SLAB, 128), blk),
            scratch_shapes=[pltpu.VMEM((D, 2 * dff), BF16), pltpu.VMEM((dff, D), BF16)],
        ),
        out_shape=jax.ShapeDtypeStruct((P * SLAB, 128), F32),
        compiler_params=_cparams(("arbitrary",)),
        name="moe_experts",
    )(block_exp, n_used, xs, wgu, bgu, wd, bd)


def _combine_kernel(dest_hbm, h_ref, gate_ref, ys_hbm, o_ref, buf, idx, sem_idx, sem,
                    *, tm, nt, grp):
    i = pl.program_id(0)
    n_idx = TOP_K * tm

    def fetch(tile):
        slot = lax.rem(tile, 3)
        gslot = lax.rem(tile // grp, 2)
        off = lax.rem(tile, grp) * n_idx

        def body(r, c):
            for k in range(TOP_K):
                d = idx[gslot, off + k * tm + r]
                pltpu.make_async_copy(_slab(ys_hbm, d), _slab(buf.at[slot, k], r),
                                      sem.at[slot]).start(priority=k % 2)
            return c
        lax.fori_loop(0, tm, body, 0, unroll=4)

    @pl.when(i == 0)
    def _():
        first = _index_group_copy(dest_hbm, idx, sem_idx, 0)
        first.start()
        first.wait()
        fetch(0)
        if nt > 1:
            fetch(1)

    slot = lax.rem(i, 3)
    _slab_wait_all(ys_hbm, buf.at[slot, 0], sem.at[slot], n_idx)

    g = gate_ref[...]
    acc = h_ref[...]
    for k in range(TOP_K):
        acc = acc + g[:, k:k + 1] * _load_slabs(buf.at[slot, k], tm)
    o_ref[...] = acc

    @pl.when(i + 2 < nt)
    def _():
        @pl.when(lax.rem(i + 2, grp) == 0)
        def _():
            _index_group_copy(dest_hbm, idx, sem_idx, (i + 2) // grp).wait()
        fetch(i + 2)

    @pl.when((i + 3 < nt) & (lax.rem(i + 3, grp) == 0))
    def _():
        _index_group_copy(dest_hbm, idx, sem_idx, (i + 3) // grp).start()


def _combine(dest_groups, h1, gates_t, ys, tm, grp):
    T, D = h1.shape
    nt = T // tm
    assert grp >= 3 or nt <= grp
    return pl.pallas_call(
        functools.partial(_combine_kernel, tm=tm, nt=nt, grp=grp),
        grid=(nt,),
        in_specs=[
            pl.BlockSpec(memory_space=pl.ANY),
            pl.BlockSpec((tm, D), lambda i: (i, 0)),
            pl.BlockSpec((tm, TOP_K), lambda i: (i, 0)),
            pl.BlockSpec(memory_space=pl.ANY),
        ],
        out_specs=pl.BlockSpec((tm, D), lambda i: (i, 0)),
        out_shape=jax.ShapeDtypeStruct((T, D), F32),
        scratch_shapes=[pltpu.VMEM((3, TOP_K, tm * SLAB, 128), F32),
                        pltpu.SMEM((2, grp * TOP_K * tm), I32),
                        pltpu.SemaphoreType.DMA, pltpu.SemaphoreType.DMA((3,))],
        compiler_params=_cparams(("arbitrary",)),
        name="moe_combine",
    )(dest_groups, h1, gates_t, ys)


def _tile(n, pref):
    t = min(n, pref)
    assert n % t == 0, (n, pref)
    return t


def _layer(h, l, p, rel_bias):
    B, S, D = h.shape
    T = B * S
    x2 = h.reshape(T, D)
    lam_init = 0.8 - 0.6 * math.exp(-0.3 * l)

    w_in = p['w_in']
    c_low = 1024
    wm = jnp.concatenate([w_in[:, :c_low], w_in[:, c_low + GLA_RANK:]], axis=1).astype(BF16)
    wlow = jnp.pad(w_in[:, c_low:c_low + GLA_RANK], ((0, 0), (0, 128 - GLA_RANK)))
    wg2 = jnp.pad(p['w_gla_gate2'], ((0, 128 - GLA_RANK), (0, 0)))
    bg2 = p['b_gla_gate2'].reshape(1, -1)
    gidx = jnp.arange(512) // DIFF_DH
    bd = (gidx[:, None] == gidx[None, :]).astype(BF16)
    qg = jnp.tile(p['diff_q_gain'], 512 // DIFF_DH).reshape(1, 512)
    kg = jnp.tile(p['diff_k_gain'], 512 // DIFF_DH).reshape(1, 512)

    tm = _tile(T, 512)
    t = _tile(S, 512)
    assert t >= 128
    gq, gk, gv, gr, la, dq, dk, dvt = _proj(
        x2, p['norm1_gain'].reshape(1, D), wm, wlow, wg2, bg2, bd, qg, kg, tm, t)

    y_gla = _gla(gq, gk, gv, gr, la, p['gla_out_gain'].reshape(1, GLA_DV), B, S, _tile(S, 256))

    rel_bias2 = rel_bias.astype(F32) * LOG2E
    y_diff = _diff(rel_bias2[REL_BUCKETS - 1],
                   dq.reshape(B, S, 512), dk.reshape(B, S, 512), dvt,
                   _bias_tiles(rel_bias2, t),
                   p['lam_q1'].reshape(1, -1), p['lam_k1'].reshape(1, -1),
                   p['lam_q2'].reshape(1, -1), p['lam_k2'].reshape(1, -1),
                   p['diff_sub_gain'].reshape(-1, 1), B, S, t, lam_init).reshape(T, 512)

    h1, m, e_t, gate_t, rank_t, counts = _mix(
        x2, y_gla, y_diff, p['w_out'].astype(BF16), p['norm2_gain'].reshape(1, D),
        p['w_router'].T, p['b_router'].reshape(-1, 1), tm)

    counts = counts.reshape(-1).astype(I32)
    padded = (counts + MOE_BLOCK - 1) // MOE_BLOCK * MOE_BLOCK
    pad_end = jnp.cumsum(padded)
    pad_start = pad_end - padded
    n_blocks = -(-(T * TOP_K) // MOE_BLOCK) + N_EXPERTS
    P = n_blocks * MOE_BLOCK
    blk_start = jnp.arange(n_blocks, dtype=I32) * MOE_BLOCK
    block_exp = jnp.minimum(jnp.sum((pad_end[None, :] <= blk_start[:, None]).astype(I32), axis=1),
                            N_EXPERTS - 1)
    n_used = (pad_end[-1:] // MOE_BLOCK).astype(I32)
    eids = jnp.arange(N_EXPERTS, dtype=I32)
    dest = rank_t + jnp.sum(jnp.where(e_t[..., None] == eids, pad_start, 0), axis=-1)

    tmd = _tile(T, 256)
    grp = _tile(T // tmd, IDX_GROUP)
    dest_groups = dest.reshape(TOP_K, T // tmd, tmd).transpose(1, 0, 2).reshape(
        T // (tmd * grp), grp * TOP_K * tmd)

    xs = _dispatch(pad_end, padded, dest_groups, m, P, tmd, grp)
    ys = _experts(block_exp, n_used, xs, p['w_gate_up'], p['b_gate_up'].reshape(N_EXPERTS, 1, -1),
                  p['w_down'], p['b_down'].reshape(N_EXPERTS, 1, -1))
    out = _combine(dest_groups, h1, gate_t.T, ys, tmd, grp)
    return out.reshape(B, S, D)


def kernel(x, norm1_gain, w_in, w_gla_gate2, b_gla_gate2, gla_out_gain, diff_q_gain, diff_k_gain,
           lam_q1, lam_k1, lam_q2, lam_k2, diff_sub_gain, rel_bias, w_out, norm2_gain,
           w_router, b_router, w_gate_up, b_gate_up, w_down, b_down):
    stacked = dict(norm1_gain=norm1_gain, w_in=w_in, w_gla_gate2=w_gla_gate2,
                   b_gla_gate2=b_gla_gate2, gla_out_gain=gla_out_gain, diff_q_gain=diff_q_gain,
                   diff_k_gain=diff_k_gain, lam_q1=lam_q1, lam_k1=lam_k1, lam_q2=lam_q2,
                   lam_k2=lam_k2, diff_sub_gain=diff_sub_gain, w_out=w_out, norm2_gain=norm2_gain,
                   w_router=w_router, b_router=b_router, w_gate_up=w_gate_up,
                   b_gate_up=b_gate_up, w_down=w_down, b_down=b_down)
    h = x
    for l in range(norm1_gain.shape[0]):
        h = _layer(h, l, {k: v[l] for k, v in stacked.items()}, rel_bias)
    return h
```
